```python
import jax, jax.numpy as jnp
from jax import lax
import numpy as np

D_MODEL = 1024
BATCH = 8
SEQ = 2048
DEPTH = 1

N_HEADS = 8
HEAD_DIM = 64
ATTN_WIDTH = N_HEADS * HEAD_DIM
IDX_HEADS = 8
IDX_DIM = 64
TOPK_MAX = 256
Q_BLOCK = 128
POOL_WINDOWS = (2, 4, 8, 16)
POOL_GROUPS = len(POOL_WINDOWS)
POOL_GROUP_DIM = 128
POOL_WIDTH = POOL_GROUPS * POOL_GROUP_DIM
D_FF = 2816
ROPE_THETA = 10000.0
EPS = 1e-6

SPLITS = (ATTN_WIDTH, HEAD_DIM, HEAD_DIM, IDX_HEADS * IDX_DIM, IDX_DIM, IDX_HEADS,
          POOL_WIDTH, D_MODEL, D_MODEL)
D_IN = sum(SPLITS)
SPLIT_POINTS = [int(c) for c in np.cumsum(SPLITS)[:-1]]

kernel_name = "hybrid_dsa_pool_macaron_block"


def rms_norm(x, g):
    xf = x.astype(jnp.float32)
    y = xf * lax.rsqrt(jnp.mean(xf * xf, axis=-1, keepdims=True) + EPS)
    return (y * g.astype(jnp.float32)).astype(x.dtype)


def rope_tables(positions, dim):
    inv_freq = ROPE_THETA ** (-jnp.arange(0, dim, 2, dtype=jnp.float32) / dim)
    ang = positions.astype(jnp.float32)[..., None] * inv_freq
    return jnp.cos(ang), jnp.sin(ang)


def apply_rope(x, cos, sin):
    xf = x.astype(jnp.float32)
    half = xf.shape[-1] // 2
    x1, x2 = xf[..., :half], xf[..., half:]
    return jnp.concatenate([x1 * cos - x2 * sin, x2 * cos + x1 * sin], axis=-1).astype(x.dtype)


def swiglu(h, w1, w3, w2):
    return (jax.nn.silu(h @ w1) * (h @ w3)) @ w2


def dsa_attention(q, k, v, qi, ki, wi, topk):
    B, S = q.shape[0], q.shape[1]
    n_blocks = S // Q_BLOCK
    scale = HEAD_DIM ** -0.5
    idx_scale = (IDX_DIM ** -0.5) * (IDX_HEADS ** -0.5)
    key_pos = jnp.arange(S)
    gather = jax.vmap(lambda t, i: t[i])

    def block(bi):
        start = bi * Q_BLOCK
        qb = lax.dynamic_slice_in_dim(q, start, Q_BLOCK, axis=1)
        qib = lax.dynamic_slice_in_dim(qi, start, Q_BLOCK, axis=1)
        wib = lax.dynamic_slice_in_dim(wi, start, Q_BLOCK, axis=1)
        qpos = start + jnp.arange(Q_BLOCK)
        causal = key_pos[None, :] <= qpos[:, None]
        dots = jnp.einsum('bqhd,bsd->bqhs', qib, ki).astype(jnp.float32)
        score = jnp.einsum('bqhs,bqh->bqs', jax.nn.relu(dots), wib.astype(jnp.float32)) * idx_scale
        score = jnp.where(causal[None], score, -jnp.inf)
        _, sel = lax.top_k(score, topk)
        valid = sel <= qpos[None, :, None]
        ks = gather(k, sel)
        vs = gather(v, sel)
        logits = jnp.einsum('bqhd,bqkd->bqhk', qb, ks).astype(jnp.float32) * scale
        logits = jnp.where(valid[:, :, None, :], logits, -jnp.inf)
        p = jax.nn.softmax(logits, axis=-1).astype(v.dtype)
        return jnp.einsum('bqhk,bqkd->bqhd', p, vs)

    out = lax.map(block, jnp.arange(n_blocks))
    return out.transpose(1, 0, 2, 3, 4).reshape(B, S, N_HEADS * HEAD_DIM)


def multiscale_pool(u, w, b, scale):
    B, S = u.shape[0], u.shape[1]
    ug = u.reshape(B, S, POOL_GROUPS, POOL_GROUP_DIM)
    count_base = jnp.arange(1, S + 1, dtype=jnp.float32)[None, :, None]
    outs = []
    for g, win in enumerate(POOL_WINDOWS):
        xg = ug[:, :, g].astype(jnp.float32)
        c = jnp.cumsum(xg, axis=1)
        lag = jnp.pad(c[:, :-win], ((0, 0), (win, 0), (0, 0)))
        mean = (c - lag) / jnp.minimum(count_base, float(win))
        outs.append(mean - xg)
    pooled = jnp.stack(outs, axis=2).astype(u.dtype)
    mixed = jnp.einsum('bsgc,gcd->bsgd', pooled, w) + b
    return mixed.reshape(B, S, POOL_WIDTH) * scale


def setup_inputs(seed: int = 0) -> dict:
    key = jax.random.key(seed)
    ks = jax.random.split(key, 24)
    L = DEPTH
    f32 = jnp.float32

    def nrm(k, shape, fan_in):
        return jax.random.normal(k, shape, f32) * (fan_in ** -0.5)

    def gain(k, shape):
        return 1.0 + 0.02 * jax.random.normal(k, shape, f32)

    return {
        "x": jax.random.normal(ks[0], (BATCH, SEQ, D_MODEL), f32),
        "positions": jnp.broadcast_to(jnp.arange(SEQ, dtype=jnp.int32), (BATCH, SEQ)),
        "ffn1_norm": gain(ks[1], (L, D_MODEL)),
        "ffn1_w1": nrm(ks[2], (L, D_MODEL, D_FF), D_MODEL),
        "ffn1_w3": nrm(ks[3], (L, D_MODEL, D_FF), D_MODEL),
        "ffn1_w2": nrm(ks[4], (L, D_FF, D_MODEL), D_FF),
        "mix_norm": gain(ks[5], (L, D_MODEL)),
        "w_in": nrm(ks[6], (L, D_MODEL, D_IN), D_MODEL),
        "q_norm": gain(ks[7], (L, HEAD_DIM)),
        "k_norm": gain(ks[8], (L, HEAD_DIM)),
        "pool_w": nrm(ks[9], (L, POOL_GROUPS, POOL_GROUP_DIM, POOL_GROUP_DIM), POOL_GROUP_DIM),
        "pool_b": 0.02 * jax.random.normal(ks[10], (L, POOL_GROUPS, POOL_GROUP_DIM), f32),
        "pool_scale": gain(ks[11], (L, POOL_WIDTH)),
        "proj_attn": nrm(ks[12], (L, ATTN_WIDTH, D_MODEL), ATTN_WIDTH),
        "proj_pool": nrm(ks[13], (L, POOL_WIDTH, D_MODEL), POOL_WIDTH),
        "w_out": nrm(ks[14], (L, D_MODEL, D_MODEL), D_MODEL),
        "ffn2_norm": gain(ks[15], (L, D_MODEL)),
        "ffn2_w1": nrm(ks[16], (L, D_MODEL, D_FF), D_MODEL),
        "ffn2_w3": nrm(ks[17], (L, D_MODEL, D_FF), D_MODEL),
        "ffn2_w2": nrm(ks[18], (L, D_FF, D_MODEL), D_FF),
    }


def reference(x, positions, ffn1_norm, ffn1_w1, ffn1_w3, ffn1_w2, mix_norm, w_in, q_norm, k_norm,
              pool_w, pool_b, pool_scale, proj_attn, proj_pool, w_out,
              ffn2_norm, ffn2_w1, ffn2_w3, ffn2_w2):
    B, S = x.shape[0], x.shape[1]
    topk = min(TOPK_MAX, S // 4)
    cos, sin = rope_tables(positions, HEAD_DIM)
    cos_h, sin_h = cos[:, :, None, :], sin[:, :, None, :]

    for l in range(DEPTH):
        x = x + 0.5 * swiglu(rms_norm(x, ffn1_norm[l]), ffn1_w1[l], ffn1_w3[l], ffn1_w2[l])

        h = rms_norm(x, mix_norm[l])
        z = h @ w_in[l]
        q, k, v, qi, ki, wi, u, g_attn, g_pool = jnp.split(z, SPLIT_POINTS, axis=-1)

        q = apply_rope(rms_norm(q.reshape(B, S, N_HEADS, HEAD_DIM), q_norm[l]), cos_h, sin_h)
        k = apply_rope(rms_norm(k, k_norm[l]), cos, sin)
        qi = apply_rope(qi.reshape(B, S, IDX_HEADS, IDX_DIM), cos_h, sin_h)
        ki = apply_rope(ki, cos, sin)
        y_attn = dsa_attention(q, k, v, qi, ki, wi, topk)

        y_pool = multiscale_pool(u, pool_w[l], pool_b[l], pool_scale[l])

        merged = (jax.nn.sigmoid(g_attn) * (y_attn @ proj_attn[l])
                  + jax.nn.sigmoid(g_pool) * (y_pool @ proj_pool[l]))
        x = x + merged @ w_out[l]

        x = x + 0.5 * swiglu(rms_norm(x, ffn2_norm[l]), ffn2_w1[l], ffn2_w3[l], ffn2_w2[l])
    return x
```

```python
import functools

import jax
import jax.numpy as jnp
import numpy as np
from jax import lax
from jax.experimental import pallas as pl
from jax.experimental.pallas import tpu as pltpu

D_MODEL = 1024
N_HEADS = 8
HEAD_DIM = 64
ATTN_WIDTH = N_HEADS * HEAD_DIM
IDX_HEADS = 8
IDX_DIM = 64
TOPK_MAX = 256
POOL_WINDOWS = (2, 4, 8, 16)
POOL_GROUPS = len(POOL_WINDOWS)
POOL_GROUP_DIM = 128
POOL_WIDTH = POOL_GROUPS * POOL_GROUP_DIM
D_FF = 2816
ROPE_THETA = 10000.0
EPS = 1e-6
SPLITS = (ATTN_WIDTH, HEAD_DIM, HEAD_DIM, IDX_HEADS * IDX_DIM, IDX_DIM, IDX_HEADS,
          POOL_WIDTH, D_MODEL, D_MODEL)

LANES = 128
MAX_WIN = max(POOL_WINDOWS)
INT_MIN = -(2 ** 31)
NEG_BIG = -1e30
VMEM_LIMIT = 56 * 1024 * 1024

F32 = jnp.float32
BF16 = jnp.bfloat16

C_Q, C_QI, C_KLO, C_KHI, C_VLO, C_VHI, C_KILO, C_KIHI, C_WI, C_END = (
    0, 512, 1024, 1152, 1280, 1408, 1536, 1664, 1792, 1920)


def _const_spec(shape):
    return pl.BlockSpec(shape, lambda *_: (0,) * len(shape), pipeline_mode=pl.Buffered(1))


def _rms(x, g):
    ms = jnp.mean(x * x, axis=-1, keepdims=True)
    return (x * lax.rsqrt(ms + EPS)) * g


def _dot(a, b):
    return jnp.dot(a, b, preferred_element_type=F32)


def _dot_nt(a, b):
    return lax.dot_general(a, b, (((1,), (1,)), ((), ())), preferred_element_type=F32)


def _ffn_kernel(x_ref, g_ref, w1_ref, w3_ref, w2_ref, o_ref):
    x = x_ref[...]
    h = _rms(x, g_ref[...]).astype(BF16)
    a = _dot(h, w1_ref[...])
    b = _dot(h, w3_ref[...])
    act = ((a * jax.nn.sigmoid(a)) * b).astype(BF16)
    o_ref[...] = x + 0.5 * _dot(act, w2_ref[...])


def _ffn(x2d, g, w1, w3, w2, *, tm):
    t = x2d.shape[0]
    return pl.pallas_call(
        _ffn_kernel,
        grid=(t // tm,),
        in_specs=[
            pl.BlockSpec((tm, D_MODEL), lambda i: (i, 0)),
            _const_spec((1, D_MODEL)),
            _const_spec((D_MODEL, D_FF)),
            _const_spec((D_MODEL, D_FF)),
            _const_spec((D_FF, D_MODEL)),
        ],
        out_specs=pl.BlockSpec((tm, D_MODEL), lambda i: (i, 0)),
        out_shape=jax.ShapeDtypeStruct((t, D_MODEL), F32),
        compiler_params=pltpu.CompilerParams(
            dimension_semantics=("arbitrary",), vmem_limit_bytes=VMEM_LIMIT),
        name="ffn",
    )(x2d, g, w1, w3, w2)


def _rope_slab(x, cos, sin_signed, lo32):
    rot = jnp.where(lo32, pltpu.roll(x, LANES - 32, axis=1), pltpu.roll(x, 32, axis=1))
    return x * cos + rot * sin_signed


def _proj_kernel(x_ref, g_ref, w_ref, hm_ref, qg_ref, kg_ref, cos_ref, sin_ref,
                 q_ref, qi_ref, klo_ref, khi_ref, vlo_ref, vhi_ref, kilo_ref, kihi_ref, wi_ref):
    tm = x_ref.shape[0]
    h = _rms(x_ref[...], g_ref[...]).astype(BF16)
    z = _dot(h, w_ref[...])
    cos = cos_ref[...]
    sin = sin_ref[...]
    lane = lax.broadcasted_iota(jnp.int32, (tm, LANES), 1)
    lo32 = (lane & 63) < 32
    lo64 = lane < 64

    q = z[:, C_Q:C_Q + ATTN_WIDTH]
    q_ms = _dot((q * q).astype(BF16), hm_ref[...])
    qn = (q * lax.rsqrt(q_ms + EPS)) * qg_ref[...]
    for s in range(ATTN_WIDTH // LANES):
        sl = slice(s * LANES, (s + 1) * LANES)
        q_ref[:, sl] = (_rope_slab(qn[:, sl], cos, sin, lo32) * (HEAD_DIM ** -0.5)).astype(BF16)
        qi_ref[:, sl] = _rope_slab(z[:, C_QI + s * LANES:C_QI + (s + 1) * LANES],
                                   cos, sin, lo32).astype(BF16)

    kg = kg_ref[...]
    for c0, ref in ((C_KLO, klo_ref), (C_KHI, khi_ref)):
        k = z[:, c0:c0 + LANES]
        ms = jnp.sum(k * k, axis=-1, keepdims=True) * (1.0 / HEAD_DIM)
        kn = (k * lax.rsqrt(ms + EPS)) * kg
        ref[...] = _rope_slab(kn, cos, sin, lo32).astype(BF16)
    for c0, ref in ((C_KILO, kilo_ref), (C_KIHI, kihi_ref)):
        ref[...] = _rope_slab(z[:, c0:c0 + LANES], cos, sin, lo32).astype(BF16)

    vlo_ref[...] = jnp.where(lo64, z[:, C_VLO:C_VLO + LANES], 1.0).astype(BF16)
    vhi_ref[...] = jnp.where(lo64, 1.0, z[:, C_VHI:C_VHI + LANES]).astype(BF16)
    wi_ref[...] = z[:, C_WI:C_WI + LANES]


def _proj(x2d, g, w_p, head_mean, qg, kg, cos, sin, *, tm):
    t = x2d.shape[0]
    row = lambda i: (i, 0)
    slab = lambda dt: jax.ShapeDtypeStruct((t, LANES), dt)
    wide = jax.ShapeDtypeStruct((t, ATTN_WIDTH), BF16)
    return pl.pallas_call(
        _proj_kernel,
        grid=(t // tm,),
        in_specs=[
            pl.BlockSpec((tm, D_MODEL), row),
            _const_spec((1, D_MODEL)),
            _const_spec((D_MODEL, C_END)),
            _const_spec((ATTN_WIDTH, ATTN_WIDTH)),
            _const_spec((1, ATTN_WIDTH)),
            _const_spec((1, LANES)),
            pl.BlockSpec((tm, LANES), row),
            pl.BlockSpec((tm, LANES), row),
        ],
        out_specs=[pl.BlockSpec((tm, ATTN_WIDTH), row)] * 2 + [pl.BlockSpec((tm, LANES), row)] * 7,
        out_shape=[wide, wide] + [slab(BF16)] * 6 + [slab(F32)],
        compiler_params=pltpu.CompilerParams(
            dimension_semantics=("arbitrary",), vmem_limit_bytes=VMEM_LIMIT),
        name="proj",
    )(x2d, g, w_p, head_mean, qg, kg, cos, sin)


def _attn_body(q_ref, qi_ref, wi_ref, klo_ref, khi_ref, vlo_ref, vhi_ref, kilo_ref, kihi_ref,
               y_ref, key_s, bias_s, *, sk, topk, idx_scale):
    tq = q_ref.shape[0]
    row = pl.program_id(1) * tq + lax.broadcasted_iota(jnp.int32, (tq, 1), 0)
    col = lax.broadcasted_iota(jnp.int32, (tq, sk), 1)

    wi = wi_ref[...]
    kilo = kilo_ref[0:sk, :]
    kihi = kihi_ref[0:sk, :]
    score = jnp.zeros((tq, sk), F32)
    for hd in range(IDX_HEADS):
        slab = qi_ref[:, (hd // 2) * LANES:(hd // 2 + 1) * LANES]
        d = _dot_nt(slab, kilo if hd % 2 == 0 else kihi)
        score = score + wi[:, hd:hd + 1] * jnp.maximum(d, 0.0)
    score = score * idx_scale

    bits = pltpu.bitcast(score, jnp.int32)
    key = bits ^ ((bits >> 31) & 0x7FFFFFFF)
    key_s[:, 0:sk] = jnp.where(col <= row, key, INT_MIN)

    kf = float(topk)

    def count(pred):
        return jnp.sum(jnp.where(pred, 1.0, 0.0), axis=-1, keepdims=True)

    t0 = jnp.where(count(key_s[:, 0:sk] >= 0) >= kf, 0, INT_MIN).astype(jnp.int32)

    def search(i, t):
        cand = t + jnp.left_shift(jnp.int32(1), 30 - i)
        return jnp.where(count(key_s[:, 0:sk] >= cand) >= kf, cand, t)

    thr = lax.fori_loop(0, 31, search, t0, unroll=2)

    keyv = key_s[:, 0:sk]
    n_ge = count(keyv >= thr)
    tie = jnp.where(thr > INT_MIN, jnp.where(n_ge > kf, 1.0, 0.0), 0.0)
    any_tie = jnp.max(tie) > 0.0

    def tie_cut():
        kv = key_s[:, 0:sk]
        need = kf - count(kv > thr)
        eq = jnp.where(kv == thr, 1.0, 0.0)

        def step(i, c):
            cand = c + jnp.left_shift(jnp.int32(1), 11 - i)
            cnt = jnp.sum(jnp.where(col < cand, eq, 0.0), axis=-1, keepdims=True)
            return jnp.where(cnt < need, cand, c)

        return lax.fori_loop(0, 12, step, jnp.zeros((tq, 1), jnp.int32))

    cut = lax.cond(any_tie, tie_cut, lambda: jnp.full((tq, 1), sk, jnp.int32))
    cut = jnp.minimum(cut, row)
    bias_s[:, 0:sk] = jnp.where(
        keyv > thr, 0.0, jnp.where(keyv == thr, jnp.where(col <= cut, 0.0, NEG_BIG), NEG_BIG))

    lo64 = lax.broadcasted_iota(jnp.int32, (tq, LANES), 1) < 64
    klo = klo_ref[0:sk, :]
    khi = khi_ref[0:sk, :]
    vlo = vlo_ref[0:sk, :]
    vhi = vhi_ref[0:sk, :]
    for pair in range(N_HEADS // 2):
        slab = q_ref[:, pair * LANES:(pair + 1) * LANES]
        res = []
        for kk, vv in ((klo, vlo), (khi, vhi)):
            s = _dot_nt(slab, kk) + bias_s[:, 0:sk]
            m = jnp.max(s, axis=-1, keepdims=True)
            p = jnp.exp(s - m).astype(BF16)
            o = _dot(p, vv)
            res.append(o / pltpu.roll(o, 64, axis=1))
        y_ref[:, pair * LANES:(pair + 1) * LANES] = jnp.where(lo64, res[0], res[1]).astype(BF16)


def _attn_kernel(*refs, n_cls, seq, topk, idx_scale):
    tq = refs[0].shape[0]
    blocks_per_cls = (seq // tq) // n_cls
    cls = pl.program_id(1) // blocks_per_cls
    for c in range(n_cls):
        pl.when(cls == c)(functools.partial(
            _attn_body, *refs, sk=(c + 1) * blocks_per_cls * tq, topk=topk, idx_scale=idx_scale))


def _attn(q, qi, wi, klo, khi, vlo, vhi, kilo, kihi, *, batch, seq, tq, n_cls, topk):
    nq = seq // tq
    qrow = lambda b, j: (b * nq + j, 0)
    full = lambda b, j: (b, 0)
    kspec = pl.BlockSpec((seq, LANES), full)
    return pl.pallas_call(
        functools.partial(_attn_kernel, n_cls=n_cls, seq=seq, topk=topk,
                          idx_scale=(IDX_DIM ** -0.5) * (IDX_HEADS ** -0.5)),
        grid=(batch, nq),
        in_specs=[
            pl.BlockSpec((tq, ATTN_WIDTH), qrow),
            pl.BlockSpec((tq, ATTN_WIDTH), qrow),
            pl.BlockSpec((tq, LANES), qrow),
            kspec, kspec, kspec, kspec, kspec, kspec,
        ],
        out_specs=pl.BlockSpec((tq, ATTN_WIDTH), qrow),
        out_shape=jax.ShapeDtypeStruct((batch * seq, ATTN_WIDTH), BF16),
        scratch_shapes=[pltpu.VMEM((tq, seq), jnp.int32), pltpu.VMEM((tq, seq), F32)],
        compiler_params=pltpu.CompilerParams(
            dimension_semantics=("arbitrary", "arbitrary"), vmem_limit_bytes=VMEM_LIMIT),
        name="attn",
    )(q, qi, wi, klo, khi, vlo, vhi, kilo, kihi)


def _merge_kernel(x_ref, ya_ref, g_ref, wug_ref, pw_ref, pb_ref, ps_ref, pa_ref, pp_ref, wo_ref,
                  o_ref, ubuf):
    tm = x_ref.shape[0]
    t = pl.program_id(1)

    @pl.when(t == 0)
    def _():
        ubuf[0:MAX_WIN, :] = jnp.zeros((MAX_WIN, POOL_WIDTH), F32)

    x = x_ref[...]
    h = _rms(x, g_ref[...]).astype(BF16)
    z = _dot(h, wug_ref[...])
    u = z[:, 0:POOL_WIDTH]
    g_attn = z[:, POOL_WIDTH:POOL_WIDTH + D_MODEL]
    g_pool = z[:, POOL_WIDTH + D_MODEL:POOL_WIDTH + 2 * D_MODEL]

    ubuf[MAX_WIN:MAX_WIN + tm, :] = u
    pos1 = (t * tm + 1 + lax.broadcasted_iota(jnp.int32, (tm, 1), 0)).astype(F32)
    pb = pb_ref[...]
    ps = ps_ref[...]
    y_pool = []
    for gi, win in enumerate(POOL_WINDOWS):
        sl = slice(gi * POOL_GROUP_DIM, (gi + 1) * POOL_GROUP_DIM)
        ug = u[:, sl]
        acc = ug
        for s in range(1, win):
            acc = acc + ubuf[MAX_WIN - s:MAX_WIN - s + tm, sl]
        pooled = (acc / jnp.minimum(pos1, float(win)) - ug).astype(BF16)
        mixed = _dot(pooled, pw_ref[gi]) + pb[:, sl]
        y_pool.append((mixed * ps[:, sl]).astype(BF16))
    ubuf[0:MAX_WIN, :] = ubuf[tm:tm + MAX_WIN, :]
    y_pool = jnp.concatenate(y_pool, axis=1)

    merged = (jax.nn.sigmoid(g_attn) * _dot(ya_ref[...], pa_ref[...])
              + jax.nn.sigmoid(g_pool) * _dot(y_pool, pp_ref[...]))
    o_ref[...] = x + _dot(merged.astype(BF16), wo_ref[...])


def _merge(x2d, y_attn, g, w_ug, pool_w, pool_b, pool_scale, proj_attn, proj_pool, w_out,
           *, batch, seq, tm):
    nt = seq // tm
    row = lambda b, i: (b * nt + i, 0)
    return pl.pallas_call(
        _merge_kernel,
        grid=(batch, nt),
        in_specs=[
            pl.BlockSpec((tm, D_MODEL), row),
            pl.BlockSpec((tm, ATTN_WIDTH), row),
            _const_spec((1, D_MODEL)),
            _const_spec((D_MODEL, POOL_WIDTH + 2 * D_MODEL)),
            _const_spec((POOL_GROUPS, POOL_GROUP_DIM, POOL_GROUP_DIM)),
            _const_spec((1, POOL_WIDTH)),
            _const_spec((1, POOL_WIDTH)),
            _const_spec((ATTN_WIDTH, D_MODEL)),
            _const_spec((POOL_WIDTH, D_MODEL)),
            _const_spec((D_MODEL, D_MODEL)),
        ],
        out_specs=pl.BlockSpec((tm, D_MODEL), row),
        out_shape=jax.ShapeDtypeStruct((batch * seq, D_MODEL), F32),
        scratch_shapes=[pltpu.VMEM((MAX_WIN + tm, POOL_WIDTH), F32)],
        compiler_params=pltpu.CompilerParams(
            dimension_semantics=("arbitrary", "arbitrary"), vmem_limit_bytes=VMEM_LIMIT),
        name="merge",
    )(x2d, y_attn, g, w_ug, pool_w, pool_b, pool_scale, proj_attn, proj_pool, w_out)


def _pack_proj_weight(w_in):
    pts = np.cumsum((0,) + SPLITS)
    wq, wk, wv, wqi, wki, wwi, wu, wga, wgp = (w_in[:, a:b] for a, b in zip(pts[:-1], pts[1:]))
    z64 = jnp.zeros((D_MODEL, HEAD_DIM), w_in.dtype)
    w_proj = jnp.concatenate(
        [wq, wqi, wk, z64, z64, wk, wv, z64, z64, wv, wki, z64, z64, wki,
         wwi, jnp.zeros((D_MODEL, LANES - IDX_HEADS), w_in.dtype)], axis=1)
    w_ug = jnp.concatenate([wu, wga, wgp], axis=1)
    return w_proj.astype(BF16), w_ug.astype(BF16)


def _rope_tables(positions):
    inv_freq = ROPE_THETA ** (-jnp.arange(0, HEAD_DIM, 2, dtype=F32) / HEAD_DIM)
    ang = positions.astype(F32).reshape(-1, 1) * inv_freq
    cos, sin = jnp.cos(ang), jnp.sin(ang)
    return (jnp.concatenate([cos, cos, cos, cos], axis=1),
            jnp.concatenate([-sin, sin, -sin, sin], axis=1))


def kernel(x, positions, ffn1_norm, ffn1_w1, ffn1_w3, ffn1_w2, mix_norm, w_in, q_norm, k_norm,
           pool_w, pool_b, pool_scale, proj_attn, proj_pool, w_out,
           ffn2_norm, ffn2_w1, ffn2_w3, ffn2_w2):
    batch, seq, _ = x.shape
    depth = ffn1_norm.shape[0]
    topk = min(TOPK_MAX, seq // 4)
    cos, sin = _rope_tables(positions)
    head_mean = jnp.asarray(
        np.kron(np.eye(N_HEADS), np.full((HEAD_DIM, HEAD_DIM), 1.0 / HEAD_DIM)), BF16)
    bf = lambda w: w.astype(BF16)

    xt = x.reshape(batch * seq, D_MODEL)
    for l in range(depth):
        w_proj, w_ug = _pack_proj_weight(w_in[l])
        xt = _ffn(xt, ffn1_norm[l][None], bf(ffn1_w1[l]), bf(ffn1_w3[l]), bf(ffn1_w2[l]), tm=256)
        q, qi, klo, khi, vlo, vhi, kilo, kihi, wi = _proj(
            xt, mix_norm[l][None], w_proj, head_mean,
            jnp.tile(q_norm[l], N_HEADS)[None], jnp.tile(k_norm[l], 2)[None], cos, sin, tm=256)
        y_attn = _attn(q, qi, wi, klo, khi, vlo, vhi, kilo, kihi,
                       batch=batch, seq=seq, tq=128, n_cls=4, topk=topk)
        xt = _merge(xt, y_attn, mix_norm[l][None], w_ug, bf(pool_w[l]),
                    pool_b[l].reshape(1, POOL_WIDTH), pool_scale[l][None],
                    bf(proj_attn[l]), bf(proj_pool[l]), bf(w_out[l]),
                    batch=batch, seq=seq, tm=256)
        xt = _ffn(xt, ffn2_norm[l][None], bf(ffn2_w1[l]), bf(ffn2_w3[l]), bf(ffn2_w2[l]), tm=256)
    return xt.reshape(batch, seq, D_MODEL)
```

```python
import functools
import math

import jax
import jax.numpy as jnp
import numpy as np
from jax import lax
from jax.experimental import pallas as pl
from jax.experimental.pallas import tpu as pltpu

D_MODEL = 1024
N_HEADS = 8
HEAD_DIM = 64
ATTN_WIDTH = N_HEADS * HEAD_DIM
IDX_HEADS = 8
IDX_DIM = 64
TOPK_MAX = 256
POOL_WINDOWS = (2, 4, 8, 16)
POOL_GROUPS = len(POOL_WINDOWS)
POOL_GROUP_DIM = 128
POOL_WIDTH = POOL_GROUPS * POOL_GROUP_DIM
D_FF = 2816
ROPE_THETA = 10000.0
EPS = 1e-6
SPLITS = (ATTN_WIDTH, HEAD_DIM, HEAD_DIM, IDX_HEADS * IDX_DIM, IDX_DIM, IDX_HEADS,
          POOL_WIDTH, D_MODEL, D_MODEL)

LANES = 128
SUBLANES = 8
MAX_WIN = max(POOL_WINDOWS)
INT_MIN = -(2 ** 31)
KEY_NEG_INF = INT_MIN + 0x7FFFFF
NEG_BIG = -1e30
VMEM_LIMIT = 56 * 1024 * 1024

F32 = jnp.float32
BF16 = jnp.bfloat16

C_Q, C_QI, C_KLO, C_KHI, C_V, C_KILO, C_KIHI, C_WI, C_END = (
    0, 512, 1024, 1152, 1280, 1408, 1536, 1664, 1792)


def _const_spec(shape):
    return pl.BlockSpec(shape, lambda *_: (0,) * len(shape), pipeline_mode=pl.Buffered(1))


def _rms(x, g):
    ms = jnp.mean(x * x, axis=-1, keepdims=True)
    return (x * lax.rsqrt(ms + EPS)) * g


def _dot(a, b):
    return jnp.dot(a, b, preferred_element_type=F32)


def _dot_nt(a, b):
    return lax.dot_general(a, b, (((1,), (1,)), ((), ())), preferred_element_type=F32)


def _ffn_kernel(x_ref, g_ref, w1_ref, w3_ref, w2_ref, o_ref):
    x = x_ref[...]
    h = _rms(x, g_ref[...]).astype(BF16)
    a = _dot(h, w1_ref[...])
    b = _dot(h, w3_ref[...])
    act = ((a * jax.nn.sigmoid(a)) * b).astype(BF16)
    o_ref[...] = x + 0.5 * _dot(act, w2_ref[...])


def _ffn(x2d, g, w1, w3, w2, *, tm):
    t = x2d.shape[0]
    return pl.pallas_call(
        _ffn_kernel,
        grid=(t // tm,),
        in_specs=[
            pl.BlockSpec((tm, D_MODEL), lambda i: (i, 0)),
            _const_spec((1, D_MODEL)),
            _const_spec((D_MODEL, D_FF)),
            _const_spec((D_MODEL, D_FF)),
            _const_spec((D_FF, D_MODEL)),
        ],
        out_specs=pl.BlockSpec((tm, D_MODEL), lambda i: (i, 0)),
        out_shape=jax.ShapeDtypeStruct((t, D_MODEL), F32),
        compiler_params=pltpu.CompilerParams(
            dimension_semantics=("arbitrary",), vmem_limit_bytes=VMEM_LIMIT),
        name="ffn",
    )(x2d, g, w1, w3, w2)


def _rope_slab(x, cos, sin_signed, lo32):
    rot = jnp.where(lo32, pltpu.roll(x, LANES - 32, axis=1), pltpu.roll(x, 32, axis=1))
    return x * cos + rot * sin_signed


def _proj_kernel(x_ref, g_ref, w_ref, hm_ref, qg_ref, kg_ref, cos_ref, sin_ref,
                 q_ref, qi_ref, klo_ref, khi_ref, vt_ref, kilo_ref, kihi_ref, wit_ref):
    tm = x_ref.shape[0]
    h = _rms(x_ref[...], g_ref[...]).astype(BF16)
    z = _dot(h, w_ref[...])
    cos = cos_ref[...]
    sin = sin_ref[...]
    lane = lax.broadcasted_iota(jnp.int32, (tm, LANES), 1)
    lo32 = (lane & 63) < 32

    q = z[:, C_Q:C_Q + ATTN_WIDTH]
    q_ms = _dot((q * q).astype(BF16), hm_ref[...])
    qn = (q * lax.rsqrt(q_ms + EPS)) * qg_ref[...]
    q_scale = (HEAD_DIM ** -0.5) * math.log2(math.e)
    for s in range(ATTN_WIDTH // LANES):
        sl = slice(s * LANES, (s + 1) * LANES)
        q_ref[:, sl] = (_rope_slab(qn[:, sl], cos, sin, lo32) * q_scale).astype(BF16)
        qi_ref[:, sl] = _rope_slab(z[:, C_QI + s * LANES:C_QI + (s + 1) * LANES],
                                   cos, sin, lo32).astype(BF16)

    kg = kg_ref[...]
    for c0, ref in ((C_KLO, klo_ref), (C_KHI, khi_ref)):
        k = z[:, c0:c0 + LANES]
        ms = jnp.sum(k * k, axis=-1, keepdims=True) * (1.0 / HEAD_DIM)
        kn = (k * lax.rsqrt(ms + EPS)) * kg
        ref[...] = _rope_slab(kn, cos, sin, lo32).astype(BF16)
    for c0, ref in ((C_KILO, kilo_ref), (C_KIHI, kihi_ref)):
        ref[...] = _rope_slab(z[:, c0:c0 + LANES], cos, sin, lo32).astype(BF16)

    v_aug = jnp.where(lane < HEAD_DIM, z[:, C_V:C_V + LANES], 1.0)
    vt_ref[...] = v_aug.T.astype(BF16)
    wit_ref[...] = z[:, C_WI:C_WI + LANES].T[0:IDX_HEADS, :]


def _proj(x2d, g, w_p, head_mean, qg, kg, cos, sin, *, tm):
    t = x2d.shape[0]
    row = lambda i: (i, 0)
    colb = lambda i: (0, i)
    slab = jax.ShapeDtypeStruct((t, LANES), BF16)
    wide = jax.ShapeDtypeStruct((t, ATTN_WIDTH), BF16)
    return pl.pallas_call(
        _proj_kernel,
        grid=(t // tm,),
        in_specs=[
            pl.BlockSpec((tm, D_MODEL), row),
            _const_spec((1, D_MODEL)),
            _const_spec((D_MODEL, C_END)),
            _const_spec((ATTN_WIDTH, ATTN_WIDTH)),
            _const_spec((1, ATTN_WIDTH)),
            _const_spec((1, LANES)),
            pl.BlockSpec((tm, LANES), row),
            pl.BlockSpec((tm, LANES), row),
        ],
        out_specs=[pl.BlockSpec((tm, ATTN_WIDTH), row)] * 2
                  + [pl.BlockSpec((tm, LANES), row)] * 2
                  + [pl.BlockSpec((LANES, tm), colb)]
                  + [pl.BlockSpec((tm, LANES), row)] * 2
                  + [pl.BlockSpec((IDX_HEADS, tm), colb)],
        out_shape=[wide, wide, slab, slab, jax.ShapeDtypeStruct((LANES, t), BF16), slab, slab,
                   jax.ShapeDtypeStruct((IDX_HEADS, t), F32)],
        compiler_params=pltpu.CompilerParams(
            dimension_semantics=("arbitrary",), vmem_limit_bytes=VMEM_LIMIT),
        name="proj",
    )(x2d, g, w_p, head_mean, qg, kg, cos, sin)


def _key_to_f32(key):
    return pltpu.bitcast(key ^ ((key >> 31) & 0x7FFFFFFF), F32)


def _colsum(x):
    n, tq = x.shape
    part = jnp.sum(x.reshape(n // 64, 64, tq), axis=0)
    return jnp.sum(part, axis=0, keepdims=True)


def _attn_body(q_ref, qi_ref, wit_ref, klo_ref, khi_ref, vt_ref, kilo_ref, kihi_ref,
               y_ref, sc_s, bias_s, *, sk, topk, idx_scale):
    tq = q_ref.shape[0]
    qidx = pl.program_id(1) * tq + lax.broadcasted_iota(jnp.int32, (1, tq), 1)
    kidx = lax.broadcasted_iota(jnp.int32, (sk, tq), 0)

    wit = wit_ref[...]
    kilo = kilo_ref[0:sk, :]
    kihi = kihi_ref[0:sk, :]
    score = jnp.zeros((sk, tq), F32)
    for hd in range(IDX_HEADS):
        slab = qi_ref[:, (hd // 2) * LANES:(hd // 2 + 1) * LANES]
        d = _dot_nt(kilo if hd % 2 == 0 else kihi, slab)
        score = score + wit[hd:hd + 1, :] * jnp.maximum(d, 0.0)
    sc_s[0:sk, :] = jnp.where(kidx <= qidx, score * idx_scale, -jnp.inf)

    kf = float(topk)

    def count_ge(cand):
        return _colsum(jnp.where(sc_s[0:sk, :] >= cand, 1.0, 0.0))

    t0 = jnp.where(count_ge(jnp.zeros((1, tq), F32)) >= kf, 0, INT_MIN).astype(jnp.int32)

    def search(i, t):
        cand = t + jnp.left_shift(jnp.int32(1), 30 - i)
        return jnp.where(count_ge(_key_to_f32(cand)) >= kf, cand, t)

    tkey = jnp.maximum(lax.fori_loop(0, 31, search, t0, unroll=2), KEY_NEG_INF)
    thr = _key_to_f32(tkey)

    n_ge = count_ge(thr)
    tie = jnp.where(tkey > KEY_NEG_INF, jnp.where(n_ge > kf, 1.0, 0.0), 0.0)
    any_tie = jnp.max(tie) > 0.0

    def tie_cut():
        sc = sc_s[0:sk, :]
        need = kf - _colsum(jnp.where(sc > thr, 1.0, 0.0))
        eq = jnp.where(sc == thr, 1.0, 0.0)

        def step(i, c):
            cand = c + jnp.left_shift(jnp.int32(1), 11 - i)
            cnt = _colsum(jnp.where(kidx < cand, eq, 0.0))
            return jnp.where(cnt < need, cand, c)

        return lax.fori_loop(0, 12, step, jnp.zeros((1, tq), jnp.int32))

    cut = lax.cond(any_tie, tie_cut, lambda: jnp.full((1, tq), sk, jnp.int32))
    cut = jnp.minimum(cut, qidx)
    sc = sc_s[0:sk, :]
    bias_s[0:sk, :] = jnp.where(
        sc > thr, 0.0, jnp.where(sc == thr, jnp.where(kidx <= cut, 0.0, NEG_BIG), NEG_BIG))

    klo = klo_ref[0:sk, :]
    khi = khi_ref[0:sk, :]
    vt = vt_ref[:, 0:sk]
    outs = []
    for hd in range(N_HEADS):
        slab = q_ref[:, (hd // 2) * LANES:(hd // 2 + 1) * LANES]
        s = _dot_nt(klo if hd % 2 == 0 else khi, slab) + bias_s[0:sk, :]
        m = jnp.max(s, axis=0, keepdims=True)
        p = jnp.exp2(s - m).astype(BF16)
        o = _dot(vt, p)
        outs.append(o[0:HEAD_DIM, :] / o[HEAD_DIM:HEAD_DIM + 1, :])
    y_ref[...] = jnp.concatenate(outs, axis=0).T.astype(BF16)


def _attn_kernel(*refs, n_cls, seq, topk, idx_scale):
    tq = refs[0].shape[0]
    blocks_per_cls = (seq // tq) // n_cls
    cls = pl.program_id(1) // blocks_per_cls
    for c in range(n_cls):
        pl.when(cls == c)(functools.partial(
            _attn_body, *refs, sk=(c + 1) * blocks_per_cls * tq, topk=topk, idx_scale=idx_scale))


def _attn(q, qi, wit, klo, khi, vt, kilo, kihi, *, batch, seq, tq, n_cls, topk):
    nq = seq // tq
    qrow = lambda b, j: (b * nq + j, 0)
    qcol = lambda b, j: (0, b * nq + j)
    kspec = pl.BlockSpec((seq, LANES), lambda b, j: (b, 0))
    return pl.pallas_call(
        functools.partial(_attn_kernel, n_cls=n_cls, seq=seq, topk=topk,
                          idx_scale=(IDX_DIM ** -0.5) * (IDX_HEADS ** -0.5)),
        grid=(batch, nq),
        in_specs=[
            pl.BlockSpec((tq, ATTN_WIDTH), qrow),
            pl.BlockSpec((tq, ATTN_WIDTH), qrow),
            pl.BlockSpec((IDX_HEADS, tq), qcol),
            kspec, kspec,
            pl.BlockSpec((LANES, seq), lambda b, j: (0, b)),
            kspec, kspec,
        ],
        out_specs=pl.BlockSpec((tq, ATTN_WIDTH), qrow),
        out_shape=jax.ShapeDtypeStruct((batch * seq, ATTN_WIDTH), BF16),
        scratch_shapes=[pltpu.VMEM((seq, tq), F32), pltpu.VMEM((seq, tq), F32)],
        compiler_params=pltpu.CompilerParams(
            dimension_semantics=("arbitrary", "arbitrary"), vmem_limit_bytes=VMEM_LIMIT),
        name="attn",
    )(q, qi, wit, klo, khi, vt, kilo, kihi)


def _merge_kernel(x_ref, ya_ref, g_ref, wug_ref, pw_ref, pb_ref, ps_ref, pa_ref, pp_ref, wo_ref,
                  o_ref, ubuf):
    tm = x_ref.shape[0]
    t = pl.program_id(1)

    @pl.when(t == 0)
    def _():
        ubuf[0:MAX_WIN, :] = jnp.zeros((MAX_WIN, POOL_WIDTH), F32)

    x = x_ref[...]
    h = _rms(x, g_ref[...]).astype(BF16)
    z = _dot(h, wug_ref[...])
    u = z[:, 0:POOL_WIDTH]
    g_attn = z[:, POOL_WIDTH:POOL_WIDTH + D_MODEL]
    g_pool = z[:, POOL_WIDTH + D_MODEL:POOL_WIDTH + 2 * D_MODEL]

    ubuf[MAX_WIN:MAX_WIN + tm, :] = u
    pos1 = (t * tm + 1 + lax.broadcasted_iota(jnp.int32, (tm, 1), 0)).astype(F32)
    pb = pb_ref[...]
    ps = ps_ref[...]
    y_pool = []
    for gi, win in enumerate(POOL_WINDOWS):
        sl = slice(gi * POOL_GROUP_DIM, (gi + 1) * POOL_GROUP_DIM)
        ug = u[:, sl]
        acc = ug
        for s in range(1, win):
            acc = acc + ubuf[MAX_WIN - s:MAX_WIN - s + tm, sl]
        pooled = (acc / jnp.minimum(pos1, float(win)) - ug).astype(BF16)
        mixed = _dot(pooled, pw_ref[gi]) + pb[:, sl]
        y_pool.append((mixed * ps[:, sl]).astype(BF16))
    ubuf[0:MAX_WIN, :] = ubuf[tm:tm + MAX_WIN, :]
    y_pool = jnp.concatenate(y_pool, axis=1)

    merged = (jax.nn.sigmoid(g_attn) * _dot(ya_ref[...], pa_ref[...])
              + jax.nn.sigmoid(g_pool) * _dot(y_pool, pp_ref[...]))
    o_ref[...] = x + _dot(merged.astype(BF16), wo_ref[...])


def _merge(x2d, y_attn, g, w_ug, pool_w, pool_b, pool_scale, proj_attn, proj_pool, w_out,
           *, batch, seq, tm):
    nt = seq // tm
    row = lambda b, i: (b * nt + i, 0)
    return pl.pallas_call(
        _merge_kernel,
        grid=(batch, nt),
        in_specs=[
            pl.BlockSpec((tm, D_MODEL), row),
            pl.BlockSpec((tm, ATTN_WIDTH), row),
            _const_spec((1, D_MODEL)),
            _const_spec((D_MODEL, POOL_WIDTH + 2 * D_MODEL)),
            _const_spec((POOL_GROUPS, POOL_GROUP_DIM, POOL_GROUP_DIM)),
            _const_spec((1, POOL_WIDTH)),
            _const_spec((1, POOL_WIDTH)),
            _const_spec((ATTN_WIDTH, D_MODEL)),
            _const_spec((POOL_WIDTH, D_MODEL)),
            _const_spec((D_MODEL, D_MODEL)),
        ],
        out_specs=pl.BlockSpec((tm, D_MODEL), row),
        out_shape=jax.ShapeDtypeStruct((batch * seq, D_MODEL), F32),
        scratch_shapes=[pltpu.VMEM((MAX_WIN + tm, POOL_WIDTH), F32)],
        compiler_params=pltpu.CompilerParams(
            dimension_semantics=("arbitrary", "arbitrary"), vmem_limit_bytes=VMEM_LIMIT),
        name="merge",
    )(x2d, y_attn, g, w_ug, pool_w, pool_b, pool_scale, proj_attn, proj_pool, w_out)


def _pack_proj_weight(w_in):
    pts = np.cumsum((0,) + SPLITS)
    wq, wk, wv, wqi, wki, wwi, wu, wga, wgp = (w_in[:, a:b] for a, b in zip(pts[:-1], pts[1:]))
    z64 = jnp.zeros((D_MODEL, HEAD_DIM), w_in.dtype)
    w_proj = jnp.concatenate(
        [wq, wqi, wk, z64, z64, wk, wv, z64, wki, z64, z64, wki,
         wwi, jnp.zeros((D_MODEL, LANES - IDX_HEADS), w_in.dtype)], axis=1)
    w_ug = jnp.concatenate([wu, wga, wgp], axis=1)
    return w_proj.astype(BF16), w_ug.astype(BF16)


def _rope_tables(positions):
    inv_freq = ROPE_THETA ** (-jnp.arange(0, HEAD_DIM, 2, dtype=F32) / HEAD_DIM)
    ang = positions.astype(F32).reshape(-1, 1) * inv_freq
    cos, sin = jnp.cos(ang), jnp.sin(ang)
    return (jnp.concatenate([cos, cos, cos, cos], axis=1),
            jnp.concatenate([-sin, sin, -sin, sin], axis=1))


def kernel(x, positions, ffn1_norm, ffn1_w1, ffn1_w3, ffn1_w2, mix_norm, w_in, q_norm, k_norm,
           pool_w, pool_b, pool_scale, proj_attn, proj_pool, w_out,
           ffn2_norm, ffn2_w1, ffn2_w3, ffn2_w2):
    batch, seq, _ = x.shape
    depth = ffn1_norm.shape[0]
    topk = min(TOPK_MAX, seq // 4)
    cos, sin = _rope_tables(positions)
    head_mean = jnp.asarray(
        np.kron(np.eye(N_HEADS), np.full((HEAD_DIM, HEAD_DIM), 1.0 / HEAD_DIM)), BF16)
    bf = lambda w: w.astype(BF16)

    xt = x.reshape(batch * seq, D_MODEL)
    for l in range(depth):
        w_proj, w_ug = _pack_proj_weight(w_in[l])
        xt = _ffn(xt, ffn1_norm[l][None], bf(ffn1_w1[l]), bf(ffn1_w3[l]), bf(ffn1_w2[l]), tm=256)
        q, qi, klo, khi, vt, kilo, kihi, wit = _proj(
            xt, mix_norm[l][None], w_proj, head_mean,
            jnp.tile(q_norm[l], N_HEADS)[None], jnp.tile(k_norm[l], 2)[None], cos, sin, tm=256)
        y_attn = _attn(q, qi, wit, klo, khi, vt, kilo, kihi,
                       batch=batch, seq=seq, tq=256, n_cls=4, topk=topk)
        xt = _merge(xt, y_attn, mix_norm[l][None], w_ug, bf(pool_w[l]),
                    pool_b[l].reshape(1, POOL_WIDTH), pool_scale[l][None],
                    bf(proj_attn[l]), bf(proj_pool[l]), bf(w_out[l]),
                    batch=batch, seq=seq, tm=256)
        xt = _ffn(xt, ffn2_norm[l][None], bf(ffn2_w1[l]), bf(ffn2_w3[l]), bf(ffn2_w2[l]), tm=256)
    return xt.reshape(batch, seq, D_MODEL)
```

```python
import functools
import math

import jax
import jax.numpy as jnp
import numpy as np
from jax import lax
from jax.experimental import pallas as pl
from jax.experimental.pallas import tpu as pltpu

D_MODEL = 1024
N_HEADS = 8
HEAD_DIM = 64
ATTN_WIDTH = N_HEADS * HEAD_DIM
IDX_HEADS = 8
IDX_DIM = 64
TOPK_MAX = 256
POOL_WINDOWS = (2, 4, 8, 16)
POOL_GROUPS = len(POOL_WINDOWS)
POOL_GROUP_DIM = 128
POOL_WIDTH = POOL_GROUPS * POOL_GROUP_DIM
D_FF = 2816
ROPE_THETA = 10000.0
EPS = 1e-6
SPLITS = (ATTN_WIDTH, HEAD_DIM, HEAD_DIM, IDX_HEADS * IDX_DIM, IDX_DIM, IDX_HEADS,
          POOL_WIDTH, D_MODEL, D_MODEL)

LANES = 128
SUBLANES = 8
MAX_WIN = max(POOL_WINDOWS)
INT_MIN = -(2 ** 31)
KEY_NEG_INF = INT_MIN + 0x7FFFFF
NEG_BIG = -1e30
VMEM_LIMIT = 56 * 1024 * 1024

TM_FFN = 256
TM_PROJ = 256
TM_MERGE = 256
TQ_ATTN = 256

F32 = jnp.float32
BF16 = jnp.bfloat16

C_Q, C_QI, C_KLO, C_KHI, C_V, C_KILO, C_KIHI, C_WI, C_END = (
    0, 512, 1024, 1152, 1280, 1408, 1536, 1664, 1792)


def _const_spec(shape):
    return pl.BlockSpec(shape, lambda *_: (0,) * len(shape), pipeline_mode=pl.Buffered(1))


def _rms(x, g):
    ms = jnp.mean(x * x, axis=-1, keepdims=True)
    return (x * lax.rsqrt(ms + EPS)) * g


def _dot(a, b):
    return jnp.dot(a, b, preferred_element_type=F32)


def _dot_nt(a, b):
    return lax.dot_general(a, b, (((1,), (1,)), ((), ())), preferred_element_type=F32)


def _ffn_kernel(x_ref, g_ref, w1_ref, w3_ref, w2_ref, o_ref):
    x = x_ref[...]
    h = _rms(x, g_ref[...]).astype(BF16)
    a = _dot(h, w1_ref[...])
    b = _dot(h, w3_ref[...])
    act = ((a * jax.nn.sigmoid(a)) * b).astype(BF16)
    o_ref[...] = x + 0.5 * _dot(act, w2_ref[...])


def _ffn(x2d, g, w1, w3, w2):
    t, tm = x2d.shape[0], TM_FFN
    return pl.pallas_call(
        _ffn_kernel,
        grid=(t // tm,),
        in_specs=[
            pl.BlockSpec((tm, D_MODEL), lambda i: (i, 0)),
            _const_spec((1, D_MODEL)),
            _const_spec((D_MODEL, D_FF)),
            _const_spec((D_MODEL, D_FF)),
            _const_spec((D_FF, D_MODEL)),
        ],
        out_specs=pl.BlockSpec((tm, D_MODEL), lambda i: (i, 0)),
        out_shape=jax.ShapeDtypeStruct((t, D_MODEL), F32),
        compiler_params=pltpu.CompilerParams(
            dimension_semantics=("arbitrary",), vmem_limit_bytes=VMEM_LIMIT),
        name="ffn",
    )(x2d, g, w1, w3, w2)


def _rope_slab(x, cos, sin_signed, lo32):
    rot = jnp.where(lo32, pltpu.roll(x, LANES - 32, axis=1), pltpu.roll(x, 32, axis=1))
    return x * cos + rot * sin_signed


def _proj_kernel(x_ref, g_ref, w_ref, hm_ref, qg_ref, kg_ref, cos_ref, sin_ref,
                 q_ref, qi_ref, k_ref, vt_ref, ki_ref, wit_ref):
    tm = x_ref.shape[0]
    h = _rms(x_ref[...], g_ref[...]).astype(BF16)
    z = _dot(h, w_ref[...])
    cos = cos_ref[...]
    sin = sin_ref[...]
    lane = lax.broadcasted_iota(jnp.int32, (tm, LANES), 1)
    lo32 = (lane & 63) < 32

    q = z[:, C_Q:C_Q + ATTN_WIDTH]
    q_ms = _dot((q * q).astype(BF16), hm_ref[...])
    qn = (q * lax.rsqrt(q_ms + EPS)) * qg_ref[...]
    q_scale = (HEAD_DIM ** -0.5) * math.log2(math.e)
    for s in range(ATTN_WIDTH // LANES):
        sl = slice(s * LANES, (s + 1) * LANES)
        q_ref[s] = (_rope_slab(qn[:, sl], cos, sin, lo32) * q_scale).astype(BF16)
        qi_ref[s] = _rope_slab(z[:, C_QI + s * LANES:C_QI + (s + 1) * LANES],
                               cos, sin, lo32).astype(BF16)

    kg = kg_ref[...]
    for half, c0 in enumerate((C_KLO, C_KHI)):
        k = z[:, c0:c0 + LANES]
        ms = jnp.sum(k * k, axis=-1, keepdims=True) * (1.0 / HEAD_DIM)
        kn = (k * lax.rsqrt(ms + EPS)) * kg
        k_ref[half] = _rope_slab(kn, cos, sin, lo32).astype(BF16)
    for half, c0 in enumerate((C_KILO, C_KIHI)):
        ki_ref[half] = _rope_slab(z[:, c0:c0 + LANES], cos, sin, lo32).astype(BF16)

    v_aug = jnp.where(lane < HEAD_DIM, z[:, C_V:C_V + LANES], 1.0)
    vt_ref[...] = v_aug.T.astype(BF16)
    wit_ref[...] = z[:, C_WI:C_WI + LANES].T[0:IDX_HEADS, :]


def _proj(x2d, g, w_p, head_mean, qg, kg, cos, sin):
    t, tm = x2d.shape[0], TM_PROJ
    row = lambda i: (i, 0)
    colb = lambda i: (0, i)
    n_pairs = ATTN_WIDTH // LANES
    stack = lambda n: (jax.ShapeDtypeStruct((n, t, LANES), BF16),
                       pl.BlockSpec((n, tm, LANES), lambda i: (0, i, 0)))
    (q_shape, q_spec), (k_shape, k_spec) = stack(n_pairs), stack(2)
    return pl.pallas_call(
        _proj_kernel,
        grid=(t // tm,),
        in_specs=[
            pl.BlockSpec((tm, D_MODEL), row),
            _const_spec((1, D_MODEL)),
            _const_spec((D_MODEL, C_END)),
            _const_spec((ATTN_WIDTH, ATTN_WIDTH)),
            _const_spec((1, ATTN_WIDTH)),
            _const_spec((1, LANES)),
            pl.BlockSpec((tm, LANES), row),
            pl.BlockSpec((tm, LANES), row),
        ],
        out_specs=[q_spec, q_spec, k_spec, pl.BlockSpec((LANES, tm), colb), k_spec,
                   pl.BlockSpec((IDX_HEADS, tm), colb)],
        out_shape=[q_shape, q_shape, k_shape, jax.ShapeDtypeStruct((LANES, t), BF16), k_shape,
                   jax.ShapeDtypeStruct((IDX_HEADS, t), F32)],
        compiler_params=pltpu.CompilerParams(
            dimension_semantics=("arbitrary",), vmem_limit_bytes=VMEM_LIMIT),
        name="proj",
    )(x2d, g, w_p, head_mean, qg, kg, cos, sin)


def _key_to_f32(key):
    return pltpu.bitcast(key ^ ((key >> 31) & 0x7FFFFFFF), F32)


def _colsum(x):
    n, tq = x.shape
    part = jnp.sum(x.reshape(n // 64, 64, tq), axis=0)
    return jnp.sum(part, axis=0, keepdims=True)


def _indexer_rows(ki_ref, qi_ref, wit_ref, hd, r0, rows):
    d = _dot_nt(ki_ref[hd & 1, pl.ds(r0, rows), :], qi_ref[hd >> 1])
    return wit_ref[pl.ds(hd, 1), :] * jnp.maximum(d, 0.0)


def _attn_step(q_ref, qi_ref, wit_ref, qin_ref, witn_ref, k_ref, vt_ref, ki_ref,
               y_ref, sc_s, lg_s, bias_s, *, j, nq, topk, idx_scale):
    tq = q_ref.shape[1]
    sk = (j + 1) * tq
    cur, nxt = j % 2, (j + 1) % 2
    has_next = j + 1 < nq
    skn = (j + 2) * tq
    qidx = j * tq + lax.broadcasted_iota(jnp.int32, (1, tq), 1)
    kidx = lax.broadcasted_iota(jnp.int32, (sk, tq), 0)

    if j == 0:
        raw = jnp.zeros((sk, tq), F32)
        for hd in range(IDX_HEADS):
            raw = raw + _indexer_rows(ki_ref, qi_ref, wit_ref, hd, 0, sk)
    else:
        raw = sc_s[cur, 0:sk, :]
    sc_s[cur, 0:sk, :] = jnp.where(kidx <= qidx, raw * idx_scale, -jnp.inf)
    if has_next:
        sc_s[nxt, 0:skn, :] = jnp.zeros((skn, tq), F32)

    kf = float(topk)

    def count_ge(cand):
        return _colsum(jnp.where(sc_s[cur, 0:sk, :] >= cand, 1.0, 0.0))

    def search(i, t):
        for bit in (31 - 2 * i, 30 - 2 * i):
            cand = t + jnp.left_shift(jnp.int32(1), bit)
            t = jnp.where(count_ge(_key_to_f32(cand)) >= kf, cand, t)
        hd, part = i >> 1, i & 1
        r0 = pl.multiple_of(part * (sk // 2), LANES)
        lg_s[hd, pl.ds(r0, sk // 2), :] = _dot_nt(k_ref[hd & 1, pl.ds(r0, sk // 2), :],
                                                   q_ref[hd >> 1])
        if has_next:
            r0n = pl.multiple_of(part * (skn // 2), LANES)
            sc_s[nxt, pl.ds(r0n, skn // 2), :] += _indexer_rows(
                ki_ref, qin_ref, witn_ref, hd, r0n, skn // 2)
        return t

    tkey = lax.fori_loop(0, 2 * N_HEADS, search, jnp.full((1, tq), INT_MIN, jnp.int32))
    tkey = jnp.maximum(tkey, KEY_NEG_INF)
    thr = _key_to_f32(tkey)

    n_ge = count_ge(thr)
    tie = jnp.where(tkey > KEY_NEG_INF, jnp.where(n_ge > kf, 1.0, 0.0), 0.0)
    any_tie = jnp.max(tie) > 0.0

    def tie_cut():
        sc = sc_s[cur, 0:sk, :]
        need = kf - _colsum(jnp.where(sc > thr, 1.0, 0.0))
        eq = jnp.where(sc == thr, 1.0, 0.0)

        def step(i, c):
            cand = c + jnp.left_shift(jnp.int32(1), 11 - i)
            cnt = _colsum(jnp.where(kidx < cand, eq, 0.0))
            return jnp.where(cnt < need, cand, c)

        return lax.fori_loop(0, 12, step, jnp.zeros((1, tq), jnp.int32))

    cut = lax.cond(any_tie, tie_cut, lambda: jnp.full((1, tq), sk, jnp.int32))
    cut = jnp.minimum(cut, qidx)
    sc = sc_s[cur, 0:sk, :]
    bias_s[0:sk, :] = jnp.where(
        sc > thr, 0.0, jnp.where(sc == thr, jnp.where(kidx <= cut, 0.0, NEG_BIG), NEG_BIG))

    vt = vt_ref[:, 0:sk]
    outs = []
    for hd in range(N_HEADS):
        s = lg_s[hd, 0:sk, :] + bias_s[0:sk, :]
        m = jnp.max(s, axis=0, keepdims=True)
        p = jnp.exp2(s - m).astype(BF16)
        o = _dot(vt, p)
        outs.append(o[0:HEAD_DIM, :] / o[HEAD_DIM:HEAD_DIM + 1, :])
    y_ref[...] = jnp.concatenate(outs, axis=0).T.astype(BF16)


def _attn_kernel(*refs, nq, topk, idx_scale):
    j = pl.program_id(1)
    for c in range(nq):
        pl.when(j == c)(functools.partial(
            _attn_step, *refs, j=c, nq=nq, topk=topk, idx_scale=idx_scale))


def _attn(q, qi, wit, k, vt, ki, *, batch, seq, topk):
    tq = TQ_ATTN
    assert IDX_HEADS == N_HEADS and seq % tq == 0 and tq % LANES == 0
    nq = seq // tq
    n_pairs = ATTN_WIDTH // LANES
    blk = lambda b, j: b * nq + j
    nblk = lambda b, j: b * nq + jnp.minimum(j + 1, nq - 1)
    qspec = lambda f: pl.BlockSpec((n_pairs, tq, LANES), lambda b, j: (0, f(b, j), 0))
    wspec = lambda f: pl.BlockSpec((IDX_HEADS, tq), lambda b, j: (0, f(b, j)))
    kspec = pl.BlockSpec((2, seq, LANES), lambda b, j: (0, b, 0))
    return pl.pallas_call(
        functools.partial(_attn_kernel, nq=nq, topk=topk,
                          idx_scale=(IDX_DIM ** -0.5) * (IDX_HEADS ** -0.5)),
        grid=(batch, nq),
        in_specs=[
            qspec(blk), qspec(blk), wspec(blk), qspec(nblk), wspec(nblk),
            kspec,
            pl.BlockSpec((LANES, seq), lambda b, j: (0, b)),
            kspec,
        ],
        out_specs=pl.BlockSpec((tq, ATTN_WIDTH), lambda b, j: (blk(b, j), 0)),
        out_shape=jax.ShapeDtypeStruct((batch * seq, ATTN_WIDTH), BF16),
        scratch_shapes=[pltpu.VMEM((2, seq, tq), F32), pltpu.VMEM((N_HEADS, seq, tq), F32),
                        pltpu.VMEM((seq, tq), F32)],
        compiler_params=pltpu.CompilerParams(
            dimension_semantics=("arbitrary", "arbitrary"), vmem_limit_bytes=VMEM_LIMIT),
        name="attn",
    )(q, qi, wit, qi, wit, k, vt, ki)


def _merge_kernel(x_ref, ya_ref, g_ref, wug_ref, pw_ref, pb_ref, ps_ref, pa_ref, pp_ref, wo_ref,
                  o_ref, ubuf):
    tm = x_ref.shape[0]
    t = pl.program_id(1)

    @pl.when(t == 0)
    def _():
        ubuf[0:MAX_WIN, :] = jnp.zeros((MAX_WIN, POOL_WIDTH), F32)

    x = x_ref[...]
    h = _rms(x, g_ref[...]).astype(BF16)
    z = _dot(h, wug_ref[...])
    u = z[:, 0:POOL_WIDTH]
    g_attn = z[:, POOL_WIDTH:POOL_WIDTH + D_MODEL]
    g_pool = z[:, POOL_WIDTH + D_MODEL:POOL_WIDTH + 2 * D_MODEL]

    ubuf[MAX_WIN:MAX_WIN + tm, :] = u
    pos1 = (t * tm + 1 + lax.broadcasted_iota(jnp.int32, (tm, 1), 0)).astype(F32)
    pb = pb_ref[...]
    ps = ps_ref[...]
    y_pool = []
    for gi, win in enumerate(POOL_WINDOWS):
        sl = slice(gi * POOL_GROUP_DIM, (gi + 1) * POOL_GROUP_DIM)
        ug = u[:, sl]
        acc = ug
        for s in range(1, win):
            acc = acc + ubuf[MAX_WIN - s:MAX_WIN - s + tm, sl]
        pooled = (acc / jnp.minimum(pos1, float(win)) - ug).astype(BF16)
        mixed = _dot(pooled, pw_ref[gi]) + pb[:, sl]
        y_pool.append((mixed * ps[:, sl]).astype(BF16))
    ubuf[0:MAX_WIN, :] = ubuf[tm:tm + MAX_WIN, :]
    y_pool = jnp.concatenate(y_pool, axis=1)

    merged = (jax.nn.sigmoid(g_attn) * _dot(ya_ref[...], pa_ref[...])
              + jax.nn.sigmoid(g_pool) * _dot(y_pool, pp_ref[...]))
    o_ref[...] = x + _dot(merged.astype(BF16), wo_ref[...])


def _merge(x2d, y_attn, g, w_ug, pool_w, pool_b, pool_scale, proj_attn, proj_pool, w_out,
           *, batch, seq):
    tm = TM_MERGE
    nt = seq // tm
    row = lambda b, i: (b * nt + i, 0)
    return pl.pallas_call(
        _merge_kernel,
        grid=(batch, nt),
        in_specs=[
            pl.BlockSpec((tm, D_MODEL), row),
            pl.BlockSpec((tm, ATTN_WIDTH), row),
            _const_spec((1, D_MODEL)),
            _const_spec((D_MODEL, POOL_WIDTH + 2 * D_MODEL)),
            _const_spec((POOL_GROUPS, POOL_GROUP_DIM, POOL_GROUP_DIM)),
            _const_spec((1, POOL_WIDTH)),
            _const_spec((1, POOL_WIDTH)),
            _const_spec((ATTN_WIDTH, D_MODEL)),
            _const_spec((POOL_WIDTH, D_MODEL)),
            _const_spec((D_MODEL, D_MODEL)),
        ],
        out_specs=pl.BlockSpec((tm, D_MODEL), row),
        out_shape=jax.ShapeDtypeStruct((batch * seq, D_MODEL), F32),
        scratch_shapes=[pltpu.VMEM((MAX_WIN + tm, POOL_WIDTH), F32)],
        compiler_params=pltpu.CompilerParams(
            dimension_semantics=("arbitrary", "arbitrary"), vmem_limit_bytes=VMEM_LIMIT),
        name="merge",
    )(x2d, y_attn, g, w_ug, pool_w, pool_b, pool_scale, proj_attn, proj_pool, w_out)


def _pack_proj_weight(w_in):
    pts = np.cumsum((0,) + SPLITS)
    wq, wk, wv, wqi, wki, wwi, wu, wga, wgp = (w_in[:, a:b] for a, b in zip(pts[:-1], pts[1:]))
    z64 = jnp.zeros((D_MODEL, HEAD_DIM), w_in.dtype)
    w_proj = jnp.concatenate(
        [wq, wqi, wk, z64, z64, wk, wv, z64, wki, z64, z64, wki,
         wwi, jnp.zeros((D_MODEL, LANES - IDX_HEADS), w_in.dtype)], axis=1)
    w_ug = jnp.concatenate([wu, wga, wgp], axis=1)
    return w_proj.astype(BF16), w_ug.astype(BF16)


def _rope_tables(positions):
    inv_freq = ROPE_THETA ** (-jnp.arange(0, HEAD_DIM, 2, dtype=F32) / HEAD_DIM)
    ang = positions.astype(F32).reshape(-1, 1) * inv_freq
    cos, sin = jnp.cos(ang), jnp.sin(ang)
    return (jnp.concatenate([cos, cos, cos, cos], axis=1),
            jnp.concatenate([-sin, sin, -sin, sin], axis=1))


def kernel(x, positions, ffn1_norm, ffn1_w1, ffn1_w3, ffn1_w2, mix_norm, w_in, q_norm, k_norm,
           pool_w, pool_b, pool_scale, proj_attn, proj_pool, w_out,
           ffn2_norm, ffn2_w1, ffn2_w3, ffn2_w2):
    batch, seq, _ = x.shape
    depth = ffn1_norm.shape[0]
    topk = min(TOPK_MAX, seq // 4)
    cos, sin = _rope_tables(positions)
    head_mean = jnp.asarray(
        np.kron(np.eye(N_HEADS), np.full((HEAD_DIM, HEAD_DIM), 1.0 / HEAD_DIM)), BF16)
    bf = lambda w: w.astype(BF16)

    xt = x.reshape(batch * seq, D_MODEL)
    for l in range(depth):
        w_proj, w_ug = _pack_proj_weight(w_in[l])
        xt = _ffn(xt, ffn1_norm[l][None], bf(ffn1_w1[l]), bf(ffn1_w3[l]), bf(ffn1_w2[l]))
        q, qi, k, vt, ki, wit = _proj(
            xt, mix_norm[l][None], w_proj, head_mean,
            jnp.tile(q_norm[l], N_HEADS)[None], jnp.tile(k_norm[l], 2)[None], cos, sin)
        y_attn = _attn(q, qi, wit, k, vt, ki, batch=batch, seq=seq, topk=topk)
        xt = _merge(xt, y_attn, mix_norm[l][None], w_ug, bf(pool_w[l]),
                    pool_b[l].reshape(1, POOL_WIDTH), pool_scale[l][None],
                    bf(proj_attn[l]), bf(proj_pool[l]), bf(w_out[l]), batch=batch, seq=seq)
        xt = _ffn(xt, ffn2_norm[l][None], bf(ffn2_w1[l]), bf(ffn2_w3[l]), bf(ffn2_w2[l]))
    return xt.reshape(batch, seq, D_MODEL)
```

```python
import functools
import math

import jax
import jax.numpy as jnp
import numpy as np
from jax import lax
from jax.experimental import pallas as pl
from jax.experimental.pallas import tpu as pltpu

D_MODEL = 1024
N_HEADS = 8
HEAD_DIM = 64
ATTN_WIDTH = N_HEADS * HEAD_DIM
IDX_HEADS = 8
IDX_DIM = 64
TOPK_MAX = 256
POOL_WINDOWS = (2, 4, 8, 16)
POOL_GROUPS = len(POOL_WINDOWS)
POOL_GROUP_DIM = 128
POOL_WIDTH = POOL_GROUPS * POOL_GROUP_DIM
D_FF = 2816
ROPE_THETA = 10000.0
EPS = 1e-6
SPLITS = (ATTN_WIDTH, HEAD_DIM, HEAD_DIM, IDX_HEADS * IDX_DIM, IDX_DIM, IDX_HEADS,
          POOL_WIDTH, D_MODEL, D_MODEL)

LANES = 128
SUBLANES = 8
MAX_WIN = max(POOL_WINDOWS)
INT_MIN = -(2 ** 31)
KEY_NEG_INF = INT_MIN + 0x7FFFFF
NEG_BIG = -1e30
VMEM_LIMIT = 56 * 1024 * 1024

TM_FFN = 512
TM_PROJ = 512
SUB_PROJ = 128
TM_MERGE = 512
TQ_ATTN = 256

F32 = jnp.float32
BF16 = jnp.bfloat16

_PTS = [int(c) for c in np.cumsum((0,) + SPLITS)]
C_Q, C_KV, C_QI, C_KIW, C_U = _PTS[0], _PTS[1], _PTS[3], _PTS[4], _PTS[6]
C_WI_LANE = _PTS[5] - C_KIW
C_END = C_KIW + LANES
assert (C_KV % LANES, C_QI % LANES, C_KIW % LANES) == (0, 0, 0)
assert C_QI - C_KV == 2 * HEAD_DIM == LANES and C_WI_LANE == IDX_DIM


def _const_spec(shape):
    return pl.BlockSpec(shape, lambda *_: (0,) * len(shape), pipeline_mode=pl.Buffered(1))


def _rms(x, g):
    ms = jnp.mean(x * x, axis=-1, keepdims=True)
    return (x * lax.rsqrt(ms + EPS)) * g


def _dot(a, b):
    return jnp.dot(a, b, preferred_element_type=F32)


def _dot_nt(a, b):
    return lax.dot_general(a, b, (((1,), (1,)), ((), ())), preferred_element_type=F32)


def _ffn_kernel(x_ref, g_ref, w1_ref, w3_ref, w2_ref, o_ref):
    x = x_ref[...]
    h = _rms(x, g_ref[...]).astype(BF16)
    a = _dot(h, w1_ref[...])
    b = _dot(h, w3_ref[...])
    act = ((a * jax.nn.sigmoid(a)) * b).astype(BF16)
    o_ref[...] = x + 0.5 * _dot(act, w2_ref[...])


def _ffn(x2d, g, w1, w3, w2):
    t, tm = x2d.shape[0], TM_FFN
    return pl.pallas_call(
        _ffn_kernel,
        grid=(t // tm,),
        in_specs=[
            pl.BlockSpec((tm, D_MODEL), lambda i: (i, 0)),
            _const_spec((1, D_MODEL)),
            _const_spec((D_MODEL, D_FF)),
            _const_spec((D_MODEL, D_FF)),
            _const_spec((D_FF, D_MODEL)),
        ],
        out_specs=pl.BlockSpec((tm, D_MODEL), lambda i: (i, 0)),
        out_shape=jax.ShapeDtypeStruct((t, D_MODEL), F32),
        compiler_params=pltpu.CompilerParams(
            dimension_semantics=("arbitrary",), vmem_limit_bytes=VMEM_LIMIT),
        name="ffn",
    )(x2d, g, w1, w3, w2)


QUARTER = HEAD_DIM // 2


def _quarter_interleave(a, axis=-1):
    axis = axis % a.ndim
    shp = a.shape
    n = shp[axis] // LANES
    a = a.reshape(shp[:axis] + (n, 2, 2, QUARTER) + shp[axis + 1:])
    return jnp.swapaxes(a, axis + 1, axis + 2).reshape(shp)


def _proj_rows(r0, rows, x_ref, g_ref, w_ref, hm_ref, qg_ref, kg_ref, cos_ref, sin_ref,
               q_ref, qi_ref, k_ref, vt_ref, ki_ref, wit_ref):
    rs = slice(r0, r0 + rows)
    h = _rms(x_ref[rs, :], g_ref[...]).astype(BF16)
    z = _dot(h, w_ref[...])
    cos = cos_ref[rs, :]
    sin = sin_ref[rs, :]
    lane = lax.broadcasted_iota(jnp.int32, (rows, LANES), 1)
    is_a = (lane & QUARTER) == 0

    def rope(x):
        return x * cos + pltpu.roll(x, HEAD_DIM, axis=1) * sin

    q = z[:, C_Q:C_Q + ATTN_WIDTH]
    q_ms = _dot((q * q).astype(BF16), hm_ref[...])
    qn = (q * lax.rsqrt(q_ms + EPS)) * qg_ref[...]
    q_scale = (HEAD_DIM ** -0.5) * math.log2(math.e)
    for s in range(ATTN_WIDTH // LANES):
        sl = slice(s * LANES, (s + 1) * LANES)
        q_ref[s, rs, :] = (rope(qn[:, sl]) * q_scale).astype(BF16)
        qi_ref[s, rs, :] = rope(z[:, C_QI + s * LANES:C_QI + (s + 1) * LANES]).astype(BF16)

    kv = z[:, C_KV:C_KV + LANES]
    ms = jnp.sum(jnp.where(is_a, kv * kv, 0.0), axis=-1, keepdims=True) * (1.0 / HEAD_DIM)
    k_a = jnp.where(is_a, rope((kv * lax.rsqrt(ms + EPS)) * kg_ref[...]), 0.0)
    k_ref[0, rs, :] = k_a.astype(BF16)
    k_ref[1, rs, :] = pltpu.roll(k_a, QUARTER, axis=1).astype(BF16)

    v = jnp.where(lane < QUARTER, pltpu.roll(kv, LANES - QUARTER, axis=1),
                  pltpu.roll(kv, HEAD_DIM, axis=1))
    vt_ref[:, rs] = jnp.where(lane < HEAD_DIM, v, 1.0).T.astype(BF16)

    kiw = z[:, C_KIW:C_KIW + LANES]
    ki_a = jnp.where(is_a, rope(kiw), 0.0)
    ki_ref[0, rs, :] = ki_a.astype(BF16)
    ki_ref[1, rs, :] = pltpu.roll(ki_a, QUARTER, axis=1).astype(BF16)
    wit_ref[:, rs] = kiw.T[QUARTER:QUARTER + IDX_HEADS, :]


def _proj_kernel(*refs):
    for r in range(TM_PROJ // SUB_PROJ):
        _proj_rows(r * SUB_PROJ, SUB_PROJ, *refs)


def _proj(x2d, g, w_p, head_mean, qg, kg, cos, sin):
    t, tm = x2d.shape[0], TM_PROJ
    row = lambda i: (i, 0)
    colb = lambda i: (0, i)
    n_pairs = ATTN_WIDTH // LANES
    stack = lambda n: (jax.ShapeDtypeStruct((n, t, LANES), BF16),
                       pl.BlockSpec((n, tm, LANES), lambda i: (0, i, 0)))
    (q_shape, q_spec), (k_shape, k_spec) = stack(n_pairs), stack(2)
    return pl.pallas_call(
        _proj_kernel,
        grid=(t // tm,),
        in_specs=[
            pl.BlockSpec((tm, D_MODEL), row),
            _const_spec((1, D_MODEL)),
            _const_spec((D_MODEL, C_END)),
            _const_spec((ATTN_WIDTH, ATTN_WIDTH)),
            _const_spec((1, ATTN_WIDTH)),
            _const_spec((1, LANES)),
            pl.BlockSpec((tm, LANES), row),
            pl.BlockSpec((tm, LANES), row),
        ],
        out_specs=[q_spec, q_spec, k_spec, pl.BlockSpec((LANES, tm), colb), k_spec,
                   pl.BlockSpec((IDX_HEADS, tm), colb)],
        out_shape=[q_shape, q_shape, k_shape, jax.ShapeDtypeStruct((LANES, t), BF16), k_shape,
                   jax.ShapeDtypeStruct((IDX_HEADS, t), F32)],
        compiler_params=pltpu.CompilerParams(
            dimension_semantics=("arbitrary",), vmem_limit_bytes=VMEM_LIMIT),
        name="proj",
    )(x2d, g, w_p, head_mean, qg, kg, cos, sin)


def _key_to_f32(key):
    return pltpu.bitcast(key ^ ((key >> 31) & 0x7FFFFFFF), F32)


def _colsum(x):
    n, tq = x.shape
    part = jnp.sum(x.reshape(n // 64, 64, tq), axis=0)
    return jnp.sum(part, axis=0, keepdims=True)


def _indexer_rows(ki_ref, qi_ref, wit_ref, hd, r0, rows):
    d = _dot_nt(ki_ref[hd & 1, pl.ds(r0, rows), :], qi_ref[hd >> 1])
    return wit_ref[pl.ds(hd, 1), :] * jnp.maximum(d, 0.0)


def _attn_step(q_ref, qi_ref, wit_ref, qin_ref, witn_ref, k_ref, vt_ref, ki_ref,
               y_ref, sc_s, lg_s, bias_s, *, j, nq, topk, idx_scale):
    tq = q_ref.shape[1]
    sk = (j + 1) * tq
    cur, nxt = j % 2, (j + 1) % 2
    has_next = j + 1 < nq
    skn = (j + 2) * tq
    qidx = j * tq + lax.broadcasted_iota(jnp.int32, (1, tq), 1)
    kidx = lax.broadcasted_iota(jnp.int32, (sk, tq), 0)

    if j == 0:
        raw = jnp.zeros((sk, tq), F32)
        for hd in range(IDX_HEADS):
            raw = raw + _indexer_rows(ki_ref, qi_ref, wit_ref, hd, 0, sk)
    else:
        raw = sc_s[cur, 0:sk, :]
    sc_s[cur, 0:sk, :] = jnp.where(kidx <= qidx, raw * idx_scale, -jnp.inf)
    if has_next:
        sc_s[nxt, 0:skn, :] = jnp.zeros((skn, tq), F32)

    kf = float(topk)

    def count_ge(cand):
        return _colsum(jnp.where(sc_s[cur, 0:sk, :] >= cand, 1.0, 0.0))

    def search(i, t):
        for bit in (31 - 2 * i, 30 - 2 * i):
            cand = t + jnp.left_shift(jnp.int32(1), bit)
            t = jnp.where(count_ge(_key_to_f32(cand)) >= kf, cand, t)
        hd, part = i >> 1, i & 1
        r0 = pl.multiple_of(part * (sk // 2), LANES)
        lg_s[hd, pl.ds(r0, sk // 2), :] = _dot_nt(k_ref[hd & 1, pl.ds(r0, sk // 2), :],
                                                   q_ref[hd >> 1])
        if has_next:
            r0n = pl.multiple_of(part * (skn // 2), LANES)
            sc_s[nxt, pl.ds(r0n, skn // 2), :] += _indexer_rows(
                ki_ref, qin_ref, witn_ref, hd, r0n, skn // 2)
        return t

    tkey = lax.fori_loop(0, 2 * N_HEADS, search, jnp.full((1, tq), INT_MIN, jnp.int32))
    tkey = jnp.maximum(tkey, KEY_NEG_INF)
    thr = _key_to_f32(tkey)

    n_ge = count_ge(thr)
    tie = jnp.where(tkey > KEY_NEG_INF, jnp.where(n_ge > kf, 1.0, 0.0), 0.0)
    any_tie = jnp.max(tie) > 0.0

    def tie_cut():
        sc = sc_s[cur, 0:sk, :]
        need = kf - _colsum(jnp.where(sc > thr, 1.0, 0.0))
        eq = jnp.where(sc == thr, 1.0, 0.0)

        def step(i, c):
            cand = c + jnp.left_shift(jnp.int32(1), 11 - i)
            cnt = _colsum(jnp.where(kidx < cand, eq, 0.0))
            return jnp.where(cnt < need, cand, c)

        return lax.fori_loop(0, 12, step, jnp.zeros((1, tq), jnp.int32))

    cut = lax.cond(any_tie, tie_cut, lambda: jnp.full((1, tq), sk, jnp.int32))
    cut = jnp.minimum(cut, qidx)
    sc = sc_s[cur, 0:sk, :]
    bias_s[0:sk, :] = jnp.where(
        sc > thr, 0.0, jnp.where(sc == thr, jnp.where(kidx <= cut, 0.0, NEG_BIG), NEG_BIG))

    vt = vt_ref[:, 0:sk]
    outs = []
    for hd in range(N_HEADS):
        s = lg_s[hd, 0:sk, :] + bias_s[0:sk, :]
        m = jnp.max(s, axis=0, keepdims=True)
        p = jnp.exp2(s - m).astype(BF16)
        o = _dot(vt, p)
        outs.append(o[0:HEAD_DIM, :] / o[HEAD_DIM:HEAD_DIM + 1, :])
    y_ref[...] = jnp.concatenate(outs, axis=0).T.astype(BF16)


def _attn_kernel(*refs, nq, topk, idx_scale):
    j = pl.program_id(1)
    for c in range(nq):
        pl.when(j == c)(functools.partial(
            _attn_step, *refs, j=c, nq=nq, topk=topk, idx_scale=idx_scale))


def _attn(q, qi, wit, k, vt, ki, *, batch, seq, topk):
    tq = TQ_ATTN
    assert IDX_HEADS == N_HEADS and seq % tq == 0 and tq % LANES == 0
    nq = seq // tq
    n_pairs = ATTN_WIDTH // LANES
    blk = lambda b, j: b * nq + j
    nblk = lambda b, j: b * nq + jnp.minimum(j + 1, nq - 1)
    qspec = lambda f: pl.BlockSpec((n_pairs, tq, LANES), lambda b, j: (0, f(b, j), 0))
    wspec = lambda f: pl.BlockSpec((IDX_HEADS, tq), lambda b, j: (0, f(b, j)))
    kspec = pl.BlockSpec((2, seq, LANES), lambda b, j: (0, b, 0))
    return pl.pallas_call(
        functools.partial(_attn_kernel, nq=nq, topk=topk,
                          idx_scale=(IDX_DIM ** -0.5) * (IDX_HEADS ** -0.5)),
        grid=(batch, nq),
        in_specs=[
            qspec(blk), qspec(blk), wspec(blk), qspec(nblk), wspec(nblk),
            kspec,
            pl.BlockSpec((LANES, seq), lambda b, j: (0, b)),
            kspec,
        ],
        out_specs=pl.BlockSpec((tq, ATTN_WIDTH), lambda b, j: (blk(b, j), 0)),
        out_shape=jax.ShapeDtypeStruct((batch * seq, ATTN_WIDTH), BF16),
        scratch_shapes=[pltpu.VMEM((2, seq, tq), F32), pltpu.VMEM((N_HEADS, seq, tq), F32),
                        pltpu.VMEM((seq, tq), F32)],
        compiler_params=pltpu.CompilerParams(
            dimension_semantics=("arbitrary", "arbitrary"), vmem_limit_bytes=VMEM_LIMIT),
        name="attn",
    )(q, qi, wit, qi, wit, k, vt, ki)


def _merge_kernel(x_ref, ya_ref, g_ref, wug_ref, pw_ref, pb_ref, ps_ref, pa_ref, pp_ref, wo_ref,
                  o_ref, ubuf):
    tm = x_ref.shape[0]
    t = pl.program_id(1)

    @pl.when(t == 0)
    def _():
        ubuf[0:MAX_WIN, :] = jnp.zeros((MAX_WIN, POOL_WIDTH), F32)

    x = x_ref[...]
    h = _rms(x, g_ref[...]).astype(BF16)
    z = _dot(h, wug_ref[...])
    u = z[:, 0:POOL_WIDTH]
    g_attn = z[:, POOL_WIDTH:POOL_WIDTH + D_MODEL]
    g_pool = z[:, POOL_WIDTH + D_MODEL:POOL_WIDTH + 2 * D_MODEL]

    ubuf[MAX_WIN:MAX_WIN + tm, :] = u
    pos1 = (t * tm + 1 + lax.broadcasted_iota(jnp.int32, (tm, 1), 0)).astype(F32)
    pb = pb_ref[...]
    ps = ps_ref[...]
    y_pool = []
    for gi, win in enumerate(POOL_WINDOWS):
        sl = slice(gi * POOL_GROUP_DIM, (gi + 1) * POOL_GROUP_DIM)
        ug = u[:, sl]
        acc = ug
        for s in range(1, win):
            acc = acc + ubuf[MAX_WIN - s:MAX_WIN - s + tm, sl]
        pooled = (acc / jnp.minimum(pos1, float(win)) - ug).astype(BF16)
        mixed = _dot(pooled, pw_ref[gi]) + pb[:, sl]
        y_pool.append((mixed * ps[:, sl]).astype(BF16))
    ubuf[0:MAX_WIN, :] = ubuf[tm:tm + MAX_WIN, :]
    y_pool = jnp.concatenate(y_pool, axis=1)

    merged = (jax.nn.sigmoid(g_attn) * _dot(ya_ref[...], pa_ref[...])
              + jax.nn.sigmoid(g_pool) * _dot(y_pool, pp_ref[...]))
    o_ref[...] = x + _dot(merged.astype(BF16), wo_ref[...])


def _merge(x2d, y_attn, g, w_ug, pool_w, pool_b, pool_scale, proj_attn, proj_pool, w_out,
           *, batch, seq):
    tm = TM_MERGE
    nt = seq // tm
    row = lambda b, i: (b * nt + i, 0)
    return pl.pallas_call(
        _merge_kernel,
        grid=(batch, nt),
        in_specs=[
            pl.BlockSpec((tm, D_MODEL), row),
            pl.BlockSpec((tm, ATTN_WIDTH), row),
            _const_spec((1, D_MODEL)),
            _const_spec((D_MODEL, POOL_WIDTH + 2 * D_MODEL)),
            _const_spec((POOL_GROUPS, POOL_GROUP_DIM, POOL_GROUP_DIM)),
            _const_spec((1, POOL_WIDTH)),
            _const_spec((1, POOL_WIDTH)),
            _const_spec((ATTN_WIDTH, D_MODEL)),
            _const_spec((POOL_WIDTH, D_MODEL)),
            _const_spec((D_MODEL, D_MODEL)),
        ],
        out_specs=pl.BlockSpec((tm, D_MODEL), row),
        out_shape=jax.ShapeDtypeStruct((batch * seq, D_MODEL), F32),
        scratch_shapes=[pltpu.VMEM((MAX_WIN + tm, POOL_WIDTH), F32)],
        compiler_params=pltpu.CompilerParams(
            dimension_semantics=("arbitrary", "arbitrary"), vmem_limit_bytes=VMEM_LIMIT),
        name="merge",
    )(x2d, y_attn, g, w_ug, pool_w, pool_b, pool_scale, proj_attn, proj_pool, w_out)


def _rope_tables(positions):
    inv_freq = ROPE_THETA ** (-jnp.arange(0, HEAD_DIM, 2, dtype=F32) / HEAD_DIM)
    ang = positions.astype(F32).reshape(-1, 1) * inv_freq
    cos, sin = jnp.cos(ang), jnp.sin(ang)
    return (jnp.concatenate([cos, cos, cos, cos], axis=1),
            jnp.concatenate([-sin, -sin, sin, sin], axis=1))


def kernel(x, positions, ffn1_norm, ffn1_w1, ffn1_w3, ffn1_w2, mix_norm, w_in, q_norm, k_norm,
           pool_w, pool_b, pool_scale, proj_attn, proj_pool, w_out,
           ffn2_norm, ffn2_w1, ffn2_w3, ffn2_w2):
    batch, seq, _ = x.shape
    depth = ffn1_norm.shape[0]
    topk = min(TOPK_MAX, seq // 4)
    cos, sin = _rope_tables(positions)
    head_of_lane = _quarter_interleave(np.repeat(np.arange(N_HEADS), HEAD_DIM))
    head_mean = jnp.asarray(
        (head_of_lane[:, None] == head_of_lane[None, :]) * (1.0 / HEAD_DIM), BF16)
    bf = lambda w: w.astype(BF16)

    xt = x.reshape(batch * seq, D_MODEL)
    for l in range(depth):
        w_proj = _quarter_interleave(bf(w_in[l, :, 0:C_END]))
        w_ug = bf(w_in[l, :, C_U:])
        qg = _quarter_interleave(jnp.tile(q_norm[l], N_HEADS))[None]
        kg = _quarter_interleave(jnp.concatenate([k_norm[l], jnp.ones_like(k_norm[l])]))[None]
        xt = _ffn(xt, ffn1_norm[l][None], bf(ffn1_w1[l]), bf(ffn1_w3[l]), bf(ffn1_w2[l]))
        q, qi, k, vt, ki, wit = _proj(xt, mix_norm[l][None], w_proj, head_mean, qg, kg, cos, sin)
        y_attn = _attn(q, qi, wit, k, vt, ki, batch=batch, seq=seq, topk=topk)
        xt = _merge(xt, y_attn, mix_norm[l][None], w_ug, bf(pool_w[l]),
                    pool_b[l].reshape(1, POOL_WIDTH), pool_scale[l][None],
                    bf(proj_attn[l]), bf(proj_pool[l]), bf(w_out[l]), batch=batch, seq=seq)
        xt = _ffn(xt, ffn2_norm[l][None], bf(ffn2_w1[l]), bf(ffn2_w3[l]), bf(ffn2_w2[l]))
    return xt.reshape(batch, seq, D_MODEL)
```

```python
import functools
import math

import jax
import jax.numpy as jnp
import numpy as np
from jax import lax
from jax.experimental import pallas as pl
from jax.experimental.pallas import tpu as pltpu

D_MODEL = 1024
N_HEADS = 8
HEAD_DIM = 64
ATTN_WIDTH = N_HEADS * HEAD_DIM
IDX_HEADS = 8
IDX_DIM = 64
TOPK_MAX = 256
POOL_WINDOWS = (2, 4, 8, 16)
POOL_GROUPS = len(POOL_WINDOWS)
POOL_GROUP_DIM = 128
POOL_WIDTH = POOL_GROUPS * POOL_GROUP_DIM
D_FF = 2816
ROPE_THETA = 10000.0
EPS = 1e-6
SPLITS = (ATTN_WIDTH, HEAD_DIM, HEAD_DIM, IDX_HEADS * IDX_DIM, IDX_DIM, IDX_HEADS,
          POOL_WIDTH, D_MODEL, D_MODEL)

LANES = 128
SUBLANES = 8
MAX_WIN = max(POOL_WINDOWS)
INT_MIN = -(2 ** 31)
KEY_NEG_INF = INT_MIN + 0x7FFFFF
NEG_BIG = -1e30
VMEM_LIMIT = 56 * 1024 * 1024

TM_FFN = 512
TM_PROJ = 512
SUB_PROJ = 128
TM_MERGE = 512
TQ_ATTN = 256
CNT_ROWS = 64

F32 = jnp.float32
BF16 = jnp.bfloat16

_PTS = [int(c) for c in np.cumsum((0,) + SPLITS)]
C_Q, C_KV, C_QI, C_KIW, C_U = _PTS[0], _PTS[1], _PTS[3], _PTS[4], _PTS[6]
C_WI_LANE = _PTS[5] - C_KIW
C_END = C_KIW + LANES
assert (C_KV % LANES, C_QI % LANES, C_KIW % LANES) == (0, 0, 0)
assert C_QI - C_KV == 2 * HEAD_DIM == LANES and C_WI_LANE == IDX_DIM


def _const_spec(shape):
    return pl.BlockSpec(shape, lambda *_: (0,) * len(shape), pipeline_mode=pl.Buffered(1))


def _rms(x, g):
    ms = jnp.mean(x * x, axis=-1, keepdims=True)
    return (x * lax.rsqrt(ms + EPS)) * g


def _dot(a, b):
    return jnp.dot(a, b, preferred_element_type=F32)


def _dot_nt(a, b):
    return lax.dot_general(a, b, (((1,), (1,)), ((), ())), preferred_element_type=F32)


def _ffn_kernel(x_ref, g_ref, w1_ref, w3_ref, w2_ref, o_ref):
    x = x_ref[...]
    h = _rms(x, g_ref[...]).astype(BF16)
    a = _dot(h, w1_ref[...])
    b = _dot(h, w3_ref[...])
    act = ((a * jax.nn.sigmoid(a)) * b).astype(BF16)
    o_ref[...] = x + 0.5 * _dot(act, w2_ref[...])


def _ffn(x2d, g, w1, w3, w2):
    t, tm = x2d.shape[0], TM_FFN
    return pl.pallas_call(
        _ffn_kernel,
        grid=(t // tm,),
        in_specs=[
            pl.BlockSpec((tm, D_MODEL), lambda i: (i, 0)),
            _const_spec((1, D_MODEL)),
            _const_spec((D_MODEL, D_FF)),
            _const_spec((D_MODEL, D_FF)),
            _const_spec((D_FF, D_MODEL)),
        ],
        out_specs=pl.BlockSpec((tm, D_MODEL), lambda i: (i, 0)),
        out_shape=jax.ShapeDtypeStruct((t, D_MODEL), F32),
        compiler_params=pltpu.CompilerParams(
            dimension_semantics=("arbitrary",), vmem_limit_bytes=VMEM_LIMIT),
        name="ffn",
    )(x2d, g, w1, w3, w2)


QUARTER = HEAD_DIM // 2


def _quarter_interleave(a, axis=-1):
    axis = axis % a.ndim
    shp = a.shape
    n = shp[axis] // LANES
    a = a.reshape(shp[:axis] + (n, 2, 2, QUARTER) + shp[axis + 1:])
    return jnp.swapaxes(a, axis + 1, axis + 2).reshape(shp)


def _proj_rows(r0, rows, x_ref, g_ref, w_ref, hm_ref, qg_ref, kg_ref, cos_ref, sin_ref,
               q_ref, qi_ref, k_ref, vt_ref, ki_ref, wit_ref):
    rs = slice(r0, r0 + rows)
    h = _rms(x_ref[rs, :], g_ref[...]).astype(BF16)
    z = _dot(h, w_ref[...])
    c32 = cos_ref[rs, :]
    s32 = sin_ref[rs, :]
    cos = jnp.concatenate([c32, c32, c32, c32], axis=1)
    sin = jnp.concatenate([-s32, -s32, s32, s32], axis=1)
    lane = lax.broadcasted_iota(jnp.int32, (rows, LANES), 1)
    is_a = (lane & QUARTER) == 0

    def rope(x):
        return x * cos + pltpu.roll(x, HEAD_DIM, axis=1) * sin

    q = z[:, C_Q:C_Q + ATTN_WIDTH]
    q_ms = _dot((q * q).astype(BF16), hm_ref[...])
    qn = (q * lax.rsqrt(q_ms + EPS)) * qg_ref[...]
    q_scale = (HEAD_DIM ** -0.5) * math.log2(math.e)
    for s in range(ATTN_WIDTH // LANES):
        sl = slice(s * LANES, (s + 1) * LANES)
        q_ref[s, rs, :] = (rope(qn[:, sl]) * q_scale).astype(BF16)
        qi_ref[s, rs, :] = rope(z[:, C_QI + s * LANES:C_QI + (s + 1) * LANES]).astype(BF16)

    kv = z[:, C_KV:C_KV + LANES]
    ms = jnp.sum(jnp.where(is_a, kv * kv, 0.0), axis=-1, keepdims=True) * (1.0 / HEAD_DIM)
    k_a = jnp.where(is_a, rope((kv * lax.rsqrt(ms + EPS)) * kg_ref[...]), 0.0)
    k_ref[0, rs, :] = k_a.astype(BF16)
    k_ref[1, rs, :] = pltpu.roll(k_a, QUARTER, axis=1).astype(BF16)

    v = jnp.where(lane < QUARTER, pltpu.roll(kv, LANES - QUARTER, axis=1),
                  pltpu.roll(kv, HEAD_DIM, axis=1))
    vt_ref[:, rs] = jnp.where(lane < HEAD_DIM, v, 1.0).T.astype(BF16)

    kiw = z[:, C_KIW:C_KIW + LANES]
    ki_a = jnp.where(is_a, rope(kiw), 0.0)
    ki_ref[0, rs, :] = ki_a.astype(BF16)
    ki_ref[1, rs, :] = pltpu.roll(ki_a, QUARTER, axis=1).astype(BF16)
    wit_ref[:, rs] = kiw.T[QUARTER:QUARTER + IDX_HEADS, :]


def _proj_kernel(*refs):
    for r in range(TM_PROJ // SUB_PROJ):
        _proj_rows(r * SUB_PROJ, SUB_PROJ, *refs)


def _proj(x2d, g, w_p, head_mean, qg, kg, cos, sin):
    t, tm = x2d.shape[0], TM_PROJ
    row = lambda i: (i, 0)
    colb = lambda i: (0, i)
    n_pairs = ATTN_WIDTH // LANES
    stack = lambda n: (jax.ShapeDtypeStruct((n, t, LANES), BF16),
                       pl.BlockSpec((n, tm, LANES), lambda i: (0, i, 0)))
    (q_shape, q_spec), (k_shape, k_spec) = stack(n_pairs), stack(2)
    return pl.pallas_call(
        _proj_kernel,
        grid=(t // tm,),
        in_specs=[
            pl.BlockSpec((tm, D_MODEL), row),
            _const_spec((1, D_MODEL)),
            _const_spec((D_MODEL, C_END)),
            _const_spec((ATTN_WIDTH, ATTN_WIDTH)),
            _const_spec((1, ATTN_WIDTH)),
            _const_spec((1, LANES)),
            pl.BlockSpec((tm, QUARTER), row),
            pl.BlockSpec((tm, QUARTER), row),
        ],
        out_specs=[q_spec, q_spec, k_spec, pl.BlockSpec((LANES, tm), colb), k_spec,
                   pl.BlockSpec((IDX_HEADS, tm), colb)],
        out_shape=[q_shape, q_shape, k_shape, jax.ShapeDtypeStruct((LANES, t), BF16), k_shape,
                   jax.ShapeDtypeStruct((IDX_HEADS, t), F32)],
        compiler_params=pltpu.CompilerParams(
            dimension_semantics=("arbitrary",), vmem_limit_bytes=VMEM_LIMIT),
        name="proj",
    )(x2d, g, w_p, head_mean, qg, kg, cos, sin)


def _key_to_f32(key):
    return pltpu.bitcast(key ^ ((key >> 31) & 0x7FFFFFFF), F32)


def _colsum(x):
    n, tq = x.shape
    part = jnp.sum(x.reshape(n // 64, 64, tq), axis=0)
    return jnp.sum(part, axis=0, keepdims=True)


def _indexer_rows(ki_ref, qi_ref, wit_ref, hd, r0, rows):
    d = _dot_nt(ki_ref[hd & 1, pl.ds(r0, rows), :], qi_ref[hd >> 1])
    return wit_ref[pl.ds(hd, 1), :] * jnp.maximum(d, 0.0)


def _attn_step(q_ref, qi_ref, wit_ref, qin_ref, witn_ref, k_ref, vt_ref, ki_ref,
               y_ref, sc_s, lg_s, bias_s, *, j, nq, topk, idx_scale):
    tq = q_ref.shape[1]
    sk = (j + 1) * tq
    cur, nxt = j % 2, (j + 1) % 2
    has_next = j + 1 < nq
    skn = (j + 2) * tq
    qidx = j * tq + lax.broadcasted_iota(jnp.int32, (1, tq), 1)
    kidx = lax.broadcasted_iota(jnp.int32, (sk, tq), 0)

    if j == 0:
        raw = jnp.zeros((sk, tq), F32)
        for hd in range(IDX_HEADS):
            raw = raw + _indexer_rows(ki_ref, qi_ref, wit_ref, hd, 0, sk)
    else:
        raw = sc_s[cur, 0:sk, :]
    sc_s[cur, 0:sk, :] = jnp.where(kidx <= qidx, raw * idx_scale, -jnp.inf)
    if has_next:
        sc_s[nxt, 0:skn, :] = jnp.zeros((skn, tq), F32)

    kf = float(topk)

    def count_ge(cand):
        acc = jnp.zeros((CNT_ROWS, tq), F32)
        for r in range(0, sk, CNT_ROWS):
            acc = jnp.where(sc_s[cur, r:r + CNT_ROWS, :] >= cand, acc + 1.0, acc)
        return jnp.sum(acc, axis=0, keepdims=True)

    def search(i, t):
        for bit in (31 - 2 * i, 30 - 2 * i):
            cand = t + jnp.left_shift(jnp.int32(1), bit)
            t = jnp.where(count_ge(_key_to_f32(cand)) >= kf, cand, t)
        hd, part = i >> 1, i & 1
        r0 = pl.multiple_of(part * (sk // 2), LANES)
        lg_s[hd, pl.ds(r0, sk // 2), :] = _dot_nt(k_ref[hd & 1, pl.ds(r0, sk // 2), :],
                                                   q_ref[hd >> 1])
        if has_next:
            r0n = pl.multiple_of(part * (skn // 2), LANES)
            sc_s[nxt, pl.ds(r0n, skn // 2), :] += _indexer_rows(
                ki_ref, qin_ref, witn_ref, hd, r0n, skn // 2)
        return t

    tkey = lax.fori_loop(0, 2 * N_HEADS, search, jnp.full((1, tq), INT_MIN, jnp.int32))
    tkey = jnp.maximum(tkey, KEY_NEG_INF)
    thr = _key_to_f32(tkey)

    n_ge = count_ge(thr)
    tie = jnp.where(tkey > KEY_NEG_INF, jnp.where(n_ge > kf, 1.0, 0.0), 0.0)
    any_tie = jnp.max(tie) > 0.0

    def tie_cut():
        sc = sc_s[cur, 0:sk, :]
        need = kf - _colsum(jnp.where(sc > thr, 1.0, 0.0))
        eq = jnp.where(sc == thr, 1.0, 0.0)

        def step(i, c):
            cand = c + jnp.left_shift(jnp.int32(1), 11 - i)
            cnt = _colsum(jnp.where(kidx < cand, eq, 0.0))
            return jnp.where(cnt < need, cand, c)

        return lax.fori_loop(0, 12, step, jnp.zeros((1, tq), jnp.int32))

    cut = lax.cond(any_tie, tie_cut, lambda: jnp.full((1, tq), sk, jnp.int32))
    cut = jnp.minimum(cut, qidx)
    sc = sc_s[cur, 0:sk, :]
    bias_s[0:sk, :] = jnp.where(
        sc > thr, 0.0, jnp.where(sc == thr, jnp.where(kidx <= cut, 0.0, NEG_BIG), NEG_BIG))

    vt = vt_ref[:, 0:sk]
    outs = []
    for hd in range(N_HEADS):
        s = lg_s[hd, 0:sk, :] + bias_s[0:sk, :]
        m = jnp.max(s, axis=0, keepdims=True)
        p = jnp.exp2(s - m).astype(BF16)
        o = _dot(vt, p)
        outs.append(o[0:HEAD_DIM, :] / o[HEAD_DIM:HEAD_DIM + 1, :])
    y_ref[...] = jnp.concatenate(outs, axis=0).T.astype(BF16)


def _attn_kernel(*refs, nq, topk, idx_scale):
    j = pl.program_id(1)
    for c in range(nq):
        pl.when(j == c)(functools.partial(
            _attn_step, *refs, j=c, nq=nq, topk=topk, idx_scale=idx_scale))


def _attn(q, qi, wit, k, vt, ki, *, batch, seq, topk):
    tq = TQ_ATTN
    assert IDX_HEADS == N_HEADS and seq % tq == 0 and tq % LANES == 0
    nq = seq // tq
    n_pairs = ATTN_WIDTH // LANES
    blk = lambda b, j: b * nq + j
    nblk = lambda b, j: b * nq + jnp.minimum(j + 1, nq - 1)
    qspec = lambda f: pl.BlockSpec((n_pairs, tq, LANES), lambda b, j: (0, f(b, j), 0))
    wspec = lambda f: pl.BlockSpec((IDX_HEADS, tq), lambda b, j: (0, f(b, j)))
    kspec = pl.BlockSpec((2, seq, LANES), lambda b, j: (0, b, 0))
    return pl.pallas_call(
        functools.partial(_attn_kernel, nq=nq, topk=topk,
                          idx_scale=(IDX_DIM ** -0.5) * (IDX_HEADS ** -0.5)),
        grid=(batch, nq),
        in_specs=[
            qspec(blk), qspec(blk), wspec(blk), qspec(nblk), wspec(nblk),
            kspec,
            pl.BlockSpec((LANES, seq), lambda b, j: (0, b)),
            kspec,
        ],
        out_specs=pl.BlockSpec((tq, ATTN_WIDTH), lambda b, j: (blk(b, j), 0)),
        out_shape=jax.ShapeDtypeStruct((batch * seq, ATTN_WIDTH), BF16),
        scratch_shapes=[pltpu.VMEM((2, seq, tq), F32), pltpu.VMEM((N_HEADS, seq, tq), F32),
                        pltpu.VMEM((seq, tq), F32)],
        compiler_params=pltpu.CompilerParams(
            dimension_semantics=("arbitrary", "arbitrary"), vmem_limit_bytes=VMEM_LIMIT),
        name="attn",
    )(q, qi, wit, qi, wit, k, vt, ki)


def _merge_kernel(x_ref, ya_ref, g_ref, wug_ref, pw_ref, pb_ref, ps_ref, pa_ref, pp_ref, wo_ref,
                  o_ref, ubuf):
    tm = x_ref.shape[0]
    t = pl.program_id(1)

    @pl.when(t == 0)
    def _():
        ubuf[0:MAX_WIN, :] = jnp.zeros((MAX_WIN, POOL_WIDTH), F32)

    x = x_ref[...]
    h = _rms(x, g_ref[...]).astype(BF16)
    z = _dot(h, wug_ref[...])
    u = z[:, 0:POOL_WIDTH]
    g_attn = z[:, POOL_WIDTH:POOL_WIDTH + D_MODEL]
    g_pool = z[:, POOL_WIDTH + D_MODEL:POOL_WIDTH + 2 * D_MODEL]

    ubuf[MAX_WIN:MAX_WIN + tm, :] = u
    pos1 = (t * tm + 1 + lax.broadcasted_iota(jnp.int32, (tm, 1), 0)).astype(F32)
    pb = pb_ref[...]
    ps = ps_ref[...]
    y_pool = []
    for gi, win in enumerate(POOL_WINDOWS):
        sl = slice(gi * POOL_GROUP_DIM, (gi + 1) * POOL_GROUP_DIM)
        ug = u[:, sl]
        acc = ug
        for s in range(1, win):
            acc = acc + ubuf[MAX_WIN - s:MAX_WIN - s + tm, sl]
        pooled = (acc / jnp.minimum(pos1, float(win)) - ug).astype(BF16)
        mixed = _dot(pooled, pw_ref[gi]) + pb[:, sl]
        y_pool.append((mixed * ps[:, sl]).astype(BF16))
    ubuf[0:MAX_WIN, :] = ubuf[tm:tm + MAX_WIN, :]
    y_pool = jnp.concatenate(y_pool, axis=1)

    merged = (jax.nn.sigmoid(g_attn) * _dot(ya_ref[...], pa_ref[...])
              + jax.nn.sigmoid(g_pool) * _dot(y_pool, pp_ref[...]))
    o_ref[...] = x + _dot(merged.astype(BF16), wo_ref[...])


def _merge(x2d, y_attn, g, w_ug, pool_w, pool_b, pool_scale, proj_attn, proj_pool, w_out,
           *, batch, seq):
    tm = TM_MERGE
    nt = seq // tm
    row = lambda b, i: (b * nt + i, 0)
    return pl.pallas_call(
        _merge_kernel,
        grid=(batch, nt),
        in_specs=[
            pl.BlockSpec((tm, D_MODEL), row),
            pl.BlockSpec((tm, ATTN_WIDTH), row),
            _const_spec((1, D_MODEL)),
            _const_spec((D_MODEL, POOL_WIDTH + 2 * D_MODEL)),
            _const_spec((POOL_GROUPS, POOL_GROUP_DIM, POOL_GROUP_DIM)),
            _const_spec((1, POOL_WIDTH)),
            _const_spec((1, POOL_WIDTH)),
            _const_spec((ATTN_WIDTH, D_MODEL)),
            _const_spec((POOL_WIDTH, D_MODEL)),
            _const_spec((D_MODEL, D_MODEL)),
        ],
        out_specs=pl.BlockSpec((tm, D_MODEL), row),
        out_shape=jax.ShapeDtypeStruct((batch * seq, D_MODEL), F32),
        scratch_shapes=[pltpu.VMEM((MAX_WIN + tm, POOL_WIDTH), F32)],
        compiler_params=pltpu.CompilerParams(
            dimension_semantics=("arbitrary", "arbitrary"), vmem_limit_bytes=VMEM_LIMIT),
        name="merge",
    )(x2d, y_attn, g, w_ug, pool_w, pool_b, pool_scale, proj_attn, proj_pool, w_out)


def _rope_tables(positions):
    inv_freq = ROPE_THETA ** (-jnp.arange(0, HEAD_DIM, 2, dtype=F32) / HEAD_DIM)
    per_row = LANES // QUARTER
    ang = (positions.astype(F32).reshape(-1, per_row, 1) * inv_freq).reshape(-1, LANES)
    cos, sin = lax.optimization_barrier((jnp.cos(ang), jnp.sin(ang)))
    return cos.reshape(-1, QUARTER), sin.reshape(-1, QUARTER)


def kernel(x, positions, ffn1_norm, ffn1_w1, ffn1_w3, ffn1_w2, mix_norm, w_in, q_norm, k_norm,
           pool_w, pool_b, pool_scale, proj_attn, proj_pool, w_out,
           ffn2_norm, ffn2_w1, ffn2_w3, ffn2_w2):
    batch, seq, _ = x.shape
    depth = ffn1_norm.shape[0]
    topk = min(TOPK_MAX, seq // 4)
    cos, sin = _rope_tables(positions)
    head_of_lane = _quarter_interleave(np.repeat(np.arange(N_HEADS), HEAD_DIM))
    head_mean = jnp.asarray(
        (head_of_lane[:, None] == head_of_lane[None, :]) * (1.0 / HEAD_DIM), BF16)
    bf = lambda w: w.astype(BF16)

    xt = x.reshape(batch * seq, D_MODEL)
    for l in range(depth):
        w_proj = _quarter_interleave(bf(w_in[l, :, 0:C_END]))
        w_ug = bf(w_in[l, :, C_U:])
        qg = _quarter_interleave(jnp.tile(q_norm[l], N_HEADS))[None]
        kg = _quarter_interleave(jnp.concatenate([k_norm[l], jnp.ones_like(k_norm[l])]))[None]
        xt = _ffn(xt, ffn1_norm[l][None], bf(ffn1_w1[l]), bf(ffn1_w3[l]), bf(ffn1_w2[l]))
        q, qi, k, vt, ki, wit = _proj(xt, mix_norm[l][None], w_proj, head_mean, qg, kg, cos, sin)
        y_attn = _attn(q, qi, wit, k, vt, ki, batch=batch, seq=seq, topk=topk)
        xt = _merge(xt, y_attn, mix_norm[l][None], w_ug, bf(pool_w[l]),
                    pool_b[l].reshape(1, POOL_WIDTH), pool_scale[l][None],
                    bf(proj_attn[l]), bf(proj_pool[l]), bf(w_out[l]), batch=batch, seq=seq)
        xt = _ffn(xt, ffn2_norm[l][None], bf(ffn2_w1[l]), bf(ffn2_w3[l]), bf(ffn2_w2[l]))
    return xt.reshape(batch, seq, D_MODEL)
```

```python
import functools
import math

import jax
import jax.numpy as jnp
import numpy as np
from jax import lax
from jax.experimental import pallas as pl
from jax.experimental.pallas import tpu as pltpu

D_MODEL = 1024
N_HEADS = 8
HEAD_DIM = 64
ATTN_WIDTH = N_HEADS * HEAD_DIM
IDX_HEADS = 8
IDX_DIM = 64
TOPK_MAX = 256
POOL_WINDOWS = (2, 4, 8, 16)
POOL_GROUPS = len(POOL_WINDOWS)
POOL_GROUP_DIM = 128
POOL_WIDTH = POOL_GROUPS * POOL_GROUP_DIM
D_FF = 2816
ROPE_THETA = 10000.0
EPS = 1e-6
SPLITS = (ATTN_WIDTH, HEAD_DIM, HEAD_DIM, IDX_HEADS * IDX_DIM, IDX_DIM, IDX_HEADS,
          POOL_WIDTH, D_MODEL, D_MODEL)

LANES = 128
SUBLANES = 8
MAX_WIN = max(POOL_WINDOWS)
INT_MIN = -(2 ** 31)
KEY_NEG_INF = INT_MIN + 0x7FFFFF
NEG_BIG = -1e30
Q_SCALE = (HEAD_DIM ** -0.5) * math.log2(math.e)
BOUND_MARGIN = 1.02
MAX_SAFE_LOGIT = 60.0
VMEM_LIMIT = 56 * 1024 * 1024

TM_FFN = 512
SUB_FFN = 256
TM_PROJ = 512
SUB_PROJ = 128
TM_MERGE = 512
SUB_MERGE = 256
TQ_ATTN = 256
CNT_ROWS = 64

F32 = jnp.float32
BF16 = jnp.bfloat16

_PTS = [int(c) for c in np.cumsum((0,) + SPLITS)]
C_Q, C_KV, C_QI, C_KIW, C_U = _PTS[0], _PTS[1], _PTS[3], _PTS[4], _PTS[6]
C_WI_LANE = _PTS[5] - C_KIW
C_END = C_KIW + LANES
assert (C_KV % LANES, C_QI % LANES, C_KIW % LANES) == (0, 0, 0)
assert C_QI - C_KV == 2 * HEAD_DIM == LANES and C_WI_LANE == IDX_DIM


def _const_spec(shape):
    return pl.BlockSpec(shape, lambda *_: (0,) * len(shape), pipeline_mode=pl.Buffered(1))


def _rms(x, g):
    ms = jnp.mean(x * x, axis=-1, keepdims=True)
    return (x * lax.rsqrt(ms + EPS)) * g


def _dot(a, b):
    return jnp.dot(a, b, preferred_element_type=F32)


def _dot_nt(a, b):
    return lax.dot_general(a, b, (((1,), (1,)), ((), ())), preferred_element_type=F32)


def _ffn_kernel(x_ref, g_ref, w1_ref, w3_ref, w2_ref, o_ref):
    for r in range(0, TM_FFN, SUB_FFN):
        x = x_ref[r:r + SUB_FFN, :]
        h = _rms(x, g_ref[...]).astype(BF16)
        a = _dot(h, w1_ref[...])
        b = _dot(h, w3_ref[...])
        act = ((a * jax.nn.sigmoid(a)) * b).astype(BF16)
        o_ref[r:r + SUB_FFN, :] = x + 0.5 * _dot(act, w2_ref[...])


def _ffn(x2d, g, w1, w3, w2):
    t, tm = x2d.shape[0], TM_FFN
    return pl.pallas_call(
        _ffn_kernel,
        grid=(t // tm,),
        in_specs=[
            pl.BlockSpec((tm, D_MODEL), lambda i: (i, 0)),
            _const_spec((1, D_MODEL)),
            _const_spec((D_MODEL, D_FF)),
            _const_spec((D_MODEL, D_FF)),
            _const_spec((D_FF, D_MODEL)),
        ],
        out_specs=pl.BlockSpec((tm, D_MODEL), lambda i: (i, 0)),
        out_shape=jax.ShapeDtypeStruct((t, D_MODEL), F32),
        compiler_params=pltpu.CompilerParams(
            dimension_semantics=("arbitrary",), vmem_limit_bytes=VMEM_LIMIT),
        name="ffn",
    )(x2d, g, w1, w3, w2)


QUARTER = HEAD_DIM // 2


def _quarter_interleave(a, axis=-1):
    axis = axis % a.ndim
    shp = a.shape
    n = shp[axis] // LANES
    a = a.reshape(shp[:axis] + (n, 2, 2, QUARTER) + shp[axis + 1:])
    return jnp.swapaxes(a, axis + 1, axis + 2).reshape(shp)


def _proj_rows(r0, rows, x_ref, g_ref, w_ref, hm_ref, qg_ref, kg_ref, cos_ref, sin_ref,
               q_ref, qi_ref, k_ref, vt_ref, ki_ref, wit_ref):
    rs = slice(r0, r0 + rows)
    h = _rms(x_ref[rs, :], g_ref[...]).astype(BF16)
    z = _dot(h, w_ref[...])
    c32 = cos_ref[rs, :]
    s32 = sin_ref[rs, :]
    cos = jnp.concatenate([c32, c32, c32, c32], axis=1)
    sin = jnp.concatenate([-s32, -s32, s32, s32], axis=1)
    lane = lax.broadcasted_iota(jnp.int32, (rows, LANES), 1)
    is_a = (lane & QUARTER) == 0

    def rope(x):
        return x * cos + pltpu.roll(x, HEAD_DIM, axis=1) * sin

    q = z[:, C_Q:C_Q + ATTN_WIDTH]
    q_ms = _dot((q * q).astype(BF16), hm_ref[...])
    qn = (q * lax.rsqrt(q_ms + EPS)) * qg_ref[...]
    for s in range(ATTN_WIDTH // LANES):
        sl = slice(s * LANES, (s + 1) * LANES)
        q_ref[s, rs, :] = (rope(qn[:, sl]) * Q_SCALE).astype(BF16)
        qi_ref[s, rs, :] = rope(z[:, C_QI + s * LANES:C_QI + (s + 1) * LANES]).astype(BF16)

    kv = z[:, C_KV:C_KV + LANES]
    ms = jnp.sum(jnp.where(is_a, kv * kv, 0.0), axis=-1, keepdims=True) * (1.0 / HEAD_DIM)
    k_a = jnp.where(is_a, rope((kv * lax.rsqrt(ms + EPS)) * kg_ref[...]), 0.0)
    k_ref[0, rs, :] = k_a.astype(BF16)
    k_ref[1, rs, :] = pltpu.roll(k_a, QUARTER, axis=1).astype(BF16)

    v = jnp.where(lane < QUARTER, pltpu.roll(kv, LANES - QUARTER, axis=1),
                  pltpu.roll(kv, HEAD_DIM, axis=1))
    vt_ref[:, rs] = jnp.where(lane < HEAD_DIM, v, 1.0).T.astype(BF16)

    kiw = z[:, C_KIW:C_KIW + LANES]
    ki_a = jnp.where(is_a, rope(kiw), 0.0)
    ki_ref[0, rs, :] = ki_a.astype(BF16)
    ki_ref[1, rs, :] = pltpu.roll(ki_a, QUARTER, axis=1).astype(BF16)
    wit_ref[:, rs] = kiw.T[QUARTER:QUARTER + IDX_HEADS, :]


def _proj_kernel(*refs):
    for r in range(TM_PROJ // SUB_PROJ):
        _proj_rows(r * SUB_PROJ, SUB_PROJ, *refs)


def _proj(x2d, g, w_p, head_mean, qg, kg, cos, sin):
    t, tm = x2d.shape[0], TM_PROJ
    row = lambda i: (i, 0)
    colb = lambda i: (0, i)
    n_pairs = ATTN_WIDTH // LANES
    stack = lambda n: (jax.ShapeDtypeStruct((n, t, LANES), BF16),
                       pl.BlockSpec((n, tm, LANES), lambda i: (0, i, 0)))
    (q_shape, q_spec), (k_shape, k_spec) = stack(n_pairs), stack(2)
    return pl.pallas_call(
        _proj_kernel,
        grid=(t // tm,),
        in_specs=[
            pl.BlockSpec((tm, D_MODEL), row),
            _const_spec((1, D_MODEL)),
            _const_spec((D_MODEL, C_END)),
            _const_spec((ATTN_WIDTH, ATTN_WIDTH)),
            _const_spec((1, ATTN_WIDTH)),
            _const_spec((1, LANES)),
            pl.BlockSpec((tm, QUARTER), row),
            pl.BlockSpec((tm, QUARTER), row),
        ],
        out_specs=[q_spec, q_spec, k_spec, pl.BlockSpec((LANES, tm), colb), k_spec,
                   pl.BlockSpec((IDX_HEADS, tm), colb)],
        out_shape=[q_shape, q_shape, k_shape, jax.ShapeDtypeStruct((LANES, t), BF16), k_shape,
                   jax.ShapeDtypeStruct((IDX_HEADS, t), F32)],
        compiler_params=pltpu.CompilerParams(
            dimension_semantics=("arbitrary",), vmem_limit_bytes=VMEM_LIMIT),
        name="proj",
    )(x2d, g, w_p, head_mean, qg, kg, cos, sin)


def _key_to_f32(key):
    return pltpu.bitcast(key ^ ((key >> 31) & 0x7FFFFFFF), F32)


def _colsum(x):
    n, tq = x.shape
    part = jnp.sum(x.reshape(n // 64, 64, tq), axis=0)
    return jnp.sum(part, axis=0, keepdims=True)


def _indexer_rows(ki_ref, qi_ref, wit_ref, hd, r0, rows):
    d = _dot_nt(ki_ref[hd & 1, pl.ds(r0, rows), :], qi_ref[hd >> 1])
    return wit_ref[pl.ds(hd, 1), :] * jnp.maximum(d, 0.0)


def _attn_step(lmax_ref, q_ref, qi_ref, wit_ref, qin_ref, witn_ref, k_ref, vt_ref, ki_ref,
               y_ref, sc_s, lg_s, bias_s, *, j, nq, topk, idx_scale):
    tq = q_ref.shape[1]
    sk = (j + 1) * tq
    cur, nxt = j % 2, (j + 1) % 2
    has_next = j + 1 < nq
    skn = (j + 2) * tq
    qidx = j * tq + lax.broadcasted_iota(jnp.int32, (1, tq), 1)
    kidx = lax.broadcasted_iota(jnp.int32, (sk, tq), 0)

    if j == 0:
        raw = jnp.zeros((sk, tq), F32)
        for hd in range(IDX_HEADS):
            raw = raw + _indexer_rows(ki_ref, qi_ref, wit_ref, hd, 0, sk)
    else:
        raw = sc_s[cur, 0:sk, :]
    sc_s[cur, 0:sk, :] = jnp.where(kidx <= qidx, raw * idx_scale, -jnp.inf)
    if has_next:
        sc_s[nxt, 0:skn, :] = jnp.zeros((skn, tq), F32)

    kf = float(topk)

    def count_ge(cand):
        acc = jnp.zeros((CNT_ROWS, tq), F32)
        for r in range(0, sk, CNT_ROWS):
            acc = jnp.where(sc_s[cur, r:r + CNT_ROWS, :] >= cand, acc + 1.0, acc)
        return jnp.sum(acc, axis=0, keepdims=True)

    def search(i, t):
        for bit in (31 - 2 * i, 30 - 2 * i):
            cand = t + jnp.left_shift(jnp.int32(1), bit)
            t = jnp.where(count_ge(_key_to_f32(cand)) >= kf, cand, t)
        hd, part = i >> 1, i & 1
        r0 = pl.multiple_of(part * (sk // 2), LANES)
        lg_s[hd, pl.ds(r0, sk // 2), :] = _dot_nt(k_ref[hd & 1, pl.ds(r0, sk // 2), :],
                                                   q_ref[hd >> 1])
        if has_next:
            r0n = pl.multiple_of(part * (skn // 2), LANES)
            sc_s[nxt, pl.ds(r0n, skn // 2), :] += _indexer_rows(
                ki_ref, qin_ref, witn_ref, hd, r0n, skn // 2)
        return t

    tkey = lax.fori_loop(0, 2 * N_HEADS, search, jnp.full((1, tq), INT_MIN, jnp.int32))
    tkey = jnp.maximum(tkey, KEY_NEG_INF)
    thr = _key_to_f32(tkey)

    n_ge = count_ge(thr)
    tie = jnp.where(tkey > KEY_NEG_INF, jnp.where(n_ge > kf, 1.0, 0.0), 0.0)
    any_tie = jnp.max(tie) > 0.0

    def tie_cut():
        sc = sc_s[cur, 0:sk, :]
        need = kf - _colsum(jnp.where(sc > thr, 1.0, 0.0))
        eq = jnp.where(sc == thr, 1.0, 0.0)

        def step(i, c):
            cand = c + jnp.left_shift(jnp.int32(1), 11 - i)
            cnt = _colsum(jnp.where(kidx < cand, eq, 0.0))
            return jnp.where(cnt < need, cand, c)

        return lax.fori_loop(0, 12, step, jnp.zeros((1, tq), jnp.int32))

    cut = lax.cond(any_tie, tie_cut, lambda: jnp.full((1, tq), sk, jnp.int32))
    cut = jnp.minimum(cut, qidx)
    bounded = lmax_ref[0] <= MAX_SAFE_LOGIT
    shift = jnp.where(bounded, lmax_ref[0], 0.0)
    sc = sc_s[cur, 0:sk, :]
    bias_s[0:sk, :] = jnp.where(
        sc > thr, -shift, jnp.where(sc == thr, jnp.where(kidx <= cut, -shift, NEG_BIG), NEG_BIG))

    def attend(row_max):
        vt = vt_ref[:, 0:sk]
        outs = []
        for hd in range(N_HEADS):
            s = lg_s[hd, 0:sk, :] + bias_s[0:sk, :]
            if row_max:
                s = s - jnp.max(s, axis=0, keepdims=True)
            o = _dot(vt, jnp.exp2(s).astype(BF16))
            outs.append(o[0:HEAD_DIM, :] / o[HEAD_DIM:HEAD_DIM + 1, :])
        y_ref[...] = jnp.concatenate(outs, axis=0).T.astype(BF16)

    pl.when(bounded)(functools.partial(attend, False))
    pl.when(jnp.logical_not(bounded))(functools.partial(attend, True))


def _attn_kernel(*refs, nq, topk, idx_scale):
    j = pl.program_id(1)
    for c in range(nq):
        pl.when(j == c)(functools.partial(
            _attn_step, *refs, j=c, nq=nq, topk=topk, idx_scale=idx_scale))


def _attn(logit_bound, q, qi, wit, k, vt, ki, *, batch, seq, topk):
    tq = TQ_ATTN
    assert IDX_HEADS == N_HEADS and seq % tq == 0 and tq % LANES == 0
    nq = seq // tq
    n_pairs = ATTN_WIDTH // LANES
    blk = lambda b, j: b * nq + j
    nblk = lambda b, j: b * nq + jnp.minimum(j + 1, nq - 1)
    qspec = lambda f: pl.BlockSpec((n_pairs, tq, LANES), lambda b, j: (0, f(b, j), 0))
    wspec = lambda f: pl.BlockSpec((IDX_HEADS, tq), lambda b, j: (0, f(b, j)))
    kspec = pl.BlockSpec((2, seq, LANES), lambda b, j: (0, b, 0))
    return pl.pallas_call(
        functools.partial(_attn_kernel, nq=nq, topk=topk,
                          idx_scale=(IDX_DIM ** -0.5) * (IDX_HEADS ** -0.5)),
        grid=(batch, nq),
        in_specs=[
            pl.BlockSpec(memory_space=pltpu.SMEM),
            qspec(blk), qspec(blk), wspec(blk), qspec(nblk), wspec(nblk),
            kspec,
            pl.BlockSpec((LANES, seq), lambda b, j: (0, b)),
            kspec,
        ],
        out_specs=pl.BlockSpec((tq, ATTN_WIDTH), lambda b, j: (blk(b, j), 0)),
        out_shape=jax.ShapeDtypeStruct((batch * seq, ATTN_WIDTH), BF16),
        scratch_shapes=[pltpu.VMEM((2, seq, tq), F32), pltpu.VMEM((N_HEADS, seq, tq), F32),
                        pltpu.VMEM((seq, tq), F32)],
        compiler_params=pltpu.CompilerParams(
            dimension_semantics=("arbitrary", "arbitrary"), vmem_limit_bytes=VMEM_LIMIT),
        name="attn",
    )(logit_bound, q, qi, wit, qi, wit, k, vt, ki)


def _merge_rows(r0, rows, t, x_ref, ya_ref, g_ref, wug_ref, pw_ref, pb_ref, ps_ref, pa_ref, pp_ref,
                wo_ref, o_ref, ubuf):
    rs = slice(r0, r0 + rows)
    x = x_ref[rs, :]
    h = _rms(x, g_ref[...]).astype(BF16)
    z = _dot(h, wug_ref[...])
    u = z[:, 0:POOL_WIDTH]
    g_attn = z[:, POOL_WIDTH:POOL_WIDTH + D_MODEL]
    g_pool = z[:, POOL_WIDTH + D_MODEL:POOL_WIDTH + 2 * D_MODEL]

    base = MAX_WIN + r0
    ubuf[base:base + rows, :] = u
    tm = x_ref.shape[0]
    pos1 = (t * tm + r0 + 1 + lax.broadcasted_iota(jnp.int32, (rows, 1), 0)).astype(F32)
    pb = pb_ref[...]
    ps = ps_ref[...]
    y_pool = []
    for gi, win in enumerate(POOL_WINDOWS):
        sl = slice(gi * POOL_GROUP_DIM, (gi + 1) * POOL_GROUP_DIM)
        ug = u[:, sl]
        acc = ug
        for s in range(1, win):
            acc = acc + ubuf[base - s:base - s + rows, sl]
        pooled = (acc / jnp.minimum(pos1, float(win)) - ug).astype(BF16)
        mixed = _dot(pooled, pw_ref[gi]) + pb[:, sl]
        y_pool.append((mixed * ps[:, sl]).astype(BF16))
    y_pool = jnp.concatenate(y_pool, axis=1)

    merged = (jax.nn.sigmoid(g_attn) * _dot(ya_ref[rs, :], pa_ref[...])
              + jax.nn.sigmoid(g_pool) * _dot(y_pool, pp_ref[...]))
    o_ref[rs, :] = x + _dot(merged.astype(BF16), wo_ref[...])


def _merge_kernel(*refs):
    ubuf = refs[-1]
    tm = refs[0].shape[0]
    t = pl.program_id(1)

    @pl.when(t == 0)
    def _():
        ubuf[0:MAX_WIN, :] = jnp.zeros((MAX_WIN, POOL_WIDTH), F32)

    for r in range(0, tm, SUB_MERGE):
        _merge_rows(r, SUB_MERGE, t, *refs)
    ubuf[0:MAX_WIN, :] = ubuf[tm:tm + MAX_WIN, :]


def _merge(x2d, y_attn, g, w_ug, pool_w, pool_b, pool_scale, proj_attn, proj_pool, w_out,
           *, batch, seq):
    tm = TM_MERGE
    nt = seq // tm
    row = lambda b, i: (b * nt + i, 0)
    return pl.pallas_call(
        _merge_kernel,
        grid=(batch, nt),
        in_specs=[
            pl.BlockSpec((tm, D_MODEL), row),
            pl.BlockSpec((tm, ATTN_WIDTH), row),
            _const_spec((1, D_MODEL)),
            _const_spec((D_MODEL, POOL_WIDTH + 2 * D_MODEL)),
            _const_spec((POOL_GROUPS, POOL_GROUP_DIM, POOL_GROUP_DIM)),
            _const_spec((1, POOL_WIDTH)),
            _const_spec((1, POOL_WIDTH)),
            _const_spec((ATTN_WIDTH, D_MODEL)),
            _const_spec((POOL_WIDTH, D_MODEL)),
            _const_spec((D_MODEL, D_MODEL)),
        ],
        out_specs=pl.BlockSpec((tm, D_MODEL), row),
        out_shape=jax.ShapeDtypeStruct((batch * seq, D_MODEL), F32),
        scratch_shapes=[pltpu.VMEM((MAX_WIN + tm, POOL_WIDTH), F32)],
        compiler_params=pltpu.CompilerParams(
            dimension_semantics=("arbitrary", "arbitrary"), vmem_limit_bytes=VMEM_LIMIT),
        name="merge",
    )(x2d, y_attn, g, w_ug, pool_w, pool_b, pool_scale, proj_attn, proj_pool, w_out)


def _rope_tables(positions):
    inv_freq = ROPE_THETA ** (-jnp.arange(0, HEAD_DIM, 2, dtype=F32) / HEAD_DIM)
    per_row = LANES // QUARTER
    ang = (positions.astype(F32).reshape(-1, per_row, 1) * inv_freq).reshape(-1, LANES)
    cos, sin = lax.optimization_barrier((jnp.cos(ang), jnp.sin(ang)))
    return cos.reshape(-1, QUARTER), sin.reshape(-1, QUARTER)


def kernel(x, positions, ffn1_norm, ffn1_w1, ffn1_w3, ffn1_w2, mix_norm, w_in, q_norm, k_norm,
           pool_w, pool_b, pool_scale, proj_attn, proj_pool, w_out,
           ffn2_norm, ffn2_w1, ffn2_w3, ffn2_w2):
    batch, seq, _ = x.shape
    depth = ffn1_norm.shape[0]
    topk = min(TOPK_MAX, seq // 4)
    cos, sin = _rope_tables(positions)
    head_of_lane = _quarter_interleave(np.repeat(np.arange(N_HEADS), HEAD_DIM))
    head_mean = jnp.asarray(
        (head_of_lane[:, None] == head_of_lane[None, :]) * (1.0 / HEAD_DIM), BF16)
    bf = lambda w: w.astype(BF16)

    xt = x.reshape(batch * seq, D_MODEL)
    for l in range(depth):
        w_proj = _quarter_interleave(bf(w_in[l, :, 0:C_END]))
        w_ug = bf(w_in[l, :, C_U:])
        qg = _quarter_interleave(jnp.tile(q_norm[l], N_HEADS))[None]
        kg = _quarter_interleave(jnp.concatenate([k_norm[l], jnp.ones_like(k_norm[l])]))[None]
        xt = _ffn(xt, ffn1_norm[l][None], bf(ffn1_w1[l]), bf(ffn1_w3[l]), bf(ffn1_w2[l]))
        q, qi, k, vt, ki, wit = _proj(xt, mix_norm[l][None], w_proj, head_mean, qg, kg, cos, sin)
        logit_bound = (BOUND_MARGIN * HEAD_DIM * Q_SCALE * jnp.max(jnp.abs(q_norm[l]))
                       * jnp.max(jnp.abs(k_norm[l]))).reshape(1)
        y_attn = _attn(logit_bound, q, qi, wit, k, vt, ki, batch=batch, seq=seq, topk=topk)
        xt = _merge(xt, y_attn, mix_norm[l][None], w_ug, bf(pool_w[l]),
                    pool_b[l].reshape(1, POOL_WIDTH), pool_scale[l][None],
                    bf(proj_attn[l]), bf(proj_pool[l]), bf(w_out[l]), batch=batch, seq=seq)
        xt = _ffn(xt, ffn2_norm[l][None], bf(ffn2_w1[l]), bf(ffn2_w3[l]), bf(ffn2_w2[l]))
    return xt.reshape(batch, seq, D_MODEL)
```

```python
import functools
import math

import jax
import jax.numpy as jnp
import numpy as np
from jax import lax
from jax.experimental import pallas as pl
from jax.experimental.pallas import tpu as pltpu

D_MODEL = 1024
N_HEADS = 8
HEAD_DIM = 64
ATTN_WIDTH = N_HEADS * HEAD_DIM
IDX_HEADS = 8
IDX_DIM = 64
TOPK_MAX = 256
POOL_WINDOWS = (2, 4, 8, 16)
POOL_GROUPS = len(POOL_WINDOWS)
POOL_GROUP_DIM = 128
POOL_WIDTH = POOL_GROUPS * POOL_GROUP_DIM
D_FF = 2816
ROPE_THETA = 10000.0
EPS = 1e-6
SPLITS = (ATTN_WIDTH, HEAD_DIM, HEAD_DIM, IDX_HEADS * IDX_DIM, IDX_DIM, IDX_HEADS,
          POOL_WIDTH, D_MODEL, D_MODEL)

LANES = 128
SUBLANES = 8
MAX_WIN = max(POOL_WINDOWS)
INT_MIN = -(2 ** 31)
KEY_NEG_INF = INT_MIN + 0x7FFFFF
NEG_BIG = -1e30
Q_SCALE = (HEAD_DIM ** -0.5) * math.log2(math.e)
BOUND_MARGIN = 1.02
MAX_SAFE_LOGIT = 60.0
VMEM_LIMIT = 56 * 1024 * 1024

TM_FFN = 512
SUB_FFN = 256
TM_PROJ = 512
SUB_PROJ = 128
TM_MERGE = 512
SUB_MERGE = 256
TQ_ATTN = 256
CNT_ROWS = 64

F32 = jnp.float32
BF16 = jnp.bfloat16

_PTS = [int(c) for c in np.cumsum((0,) + SPLITS)]
C_Q, C_KV, C_QI, C_KIW, C_U = _PTS[0], _PTS[1], _PTS[3], _PTS[4], _PTS[6]
C_WI_LANE = _PTS[5] - C_KIW
C_END = C_KIW + LANES
assert (C_KV % LANES, C_QI % LANES, C_KIW % LANES) == (0, 0, 0)
assert C_QI - C_KV == 2 * HEAD_DIM == LANES and C_WI_LANE == IDX_DIM


def _const_spec(shape):
    return pl.BlockSpec(shape, lambda *_: (0,) * len(shape), pipeline_mode=pl.Buffered(1))


def _rms(x, g):
    ms = jnp.mean(x * x, axis=-1, keepdims=True)
    return (x * lax.rsqrt(ms + EPS)) * g


def _dot(a, b):
    return jnp.dot(a, b, preferred_element_type=F32)


def _dot_nt(a, b):
    return lax.dot_general(a, b, (((1,), (1,)), ((), ())), preferred_element_type=F32)


def _ffn_kernel(x_ref, g_ref, w1_ref, w3_ref, w2_ref, o_ref):
    for r in range(0, TM_FFN, SUB_FFN):
        x = x_ref[r:r + SUB_FFN, :]
        h = _rms(x, g_ref[...]).astype(BF16)
        a = _dot(h, w1_ref[...])
        b = _dot(h, w3_ref[...])
        act = ((a * jax.nn.sigmoid(a)) * b).astype(BF16)
        o_ref[r:r + SUB_FFN, :] = x + 0.5 * _dot(act, w2_ref[...])


def _ffn(x2d, g, w1, w3, w2):
    t, tm = x2d.shape[0], TM_FFN
    return pl.pallas_call(
        _ffn_kernel,
        grid=(t // tm,),
        in_specs=[
            pl.BlockSpec((tm, D_MODEL), lambda i: (i, 0)),
            _const_spec((1, D_MODEL)),
            _const_spec((D_MODEL, D_FF)),
            _const_spec((D_MODEL, D_FF)),
            _const_spec((D_FF, D_MODEL)),
        ],
        out_specs=pl.BlockSpec((tm, D_MODEL), lambda i: (i, 0)),
        out_shape=jax.ShapeDtypeStruct((t, D_MODEL), F32),
        compiler_params=pltpu.CompilerParams(
            dimension_semantics=("arbitrary",), vmem_limit_bytes=VMEM_LIMIT),
        name="ffn",
    )(x2d, g, w1, w3, w2)


QUARTER = HEAD_DIM // 2


def _quarter_interleave(a, axis=-1):
    axis = axis % a.ndim
    shp = a.shape
    n = shp[axis] // LANES
    a = a.reshape(shp[:axis] + (n, 2, 2, QUARTER) + shp[axis + 1:])
    return jnp.swapaxes(a, axis + 1, axis + 2).reshape(shp)


def _proj_rows(r0, rows, x_ref, g_ref, w_ref, hm_ref, qg_ref, kg_ref, cos_ref, sin_ref,
               q_ref, qi_ref, k_ref, vt_ref, ki_ref, wit_ref):
    rs = slice(r0, r0 + rows)
    h = _rms(x_ref[rs, :], g_ref[...]).astype(BF16)
    z = _dot(h, w_ref[...])
    c32 = cos_ref[rs, :]
    s32 = sin_ref[rs, :]
    cos = jnp.concatenate([c32, c32, c32, c32], axis=1)
    sin = jnp.concatenate([-s32, -s32, s32, s32], axis=1)
    lane = lax.broadcasted_iota(jnp.int32, (rows, LANES), 1)
    is_a = (lane & QUARTER) == 0

    def rope(x):
        return x * cos + pltpu.roll(x, HEAD_DIM, axis=1) * sin

    q = z[:, C_Q:C_Q + ATTN_WIDTH]
    q_ms = _dot((q * q).astype(BF16), hm_ref[...])
    qn = (q * lax.rsqrt(q_ms + EPS)) * qg_ref[...]
    for s in range(ATTN_WIDTH // LANES):
        sl = slice(s * LANES, (s + 1) * LANES)
        q_ref[s, rs, :] = (rope(qn[:, sl]) * Q_SCALE).astype(BF16)
        qi_ref[s, rs, :] = rope(z[:, C_QI + s * LANES:C_QI + (s + 1) * LANES]).astype(BF16)

    kv = z[:, C_KV:C_KV + LANES]
    ms = jnp.sum(jnp.where(is_a, kv * kv, 0.0), axis=-1, keepdims=True) * (1.0 / HEAD_DIM)
    k_a = jnp.where(is_a, rope((kv * lax.rsqrt(ms + EPS)) * kg_ref[...]), 0.0)
    k_ref[0, rs, :] = k_a.astype(BF16)
    k_ref[1, rs, :] = pltpu.roll(k_a, QUARTER, axis=1).astype(BF16)

    v = jnp.where(lane < QUARTER, pltpu.roll(kv, LANES - QUARTER, axis=1),
                  pltpu.roll(kv, HEAD_DIM, axis=1))
    vt_ref[:, rs] = jnp.where(lane < HEAD_DIM, v, 1.0).T.astype(BF16)

    kiw = z[:, C_KIW:C_KIW + LANES]
    ki_a = jnp.where(is_a, rope(kiw), 0.0)
    ki_ref[0, rs, :] = ki_a.astype(BF16)
    ki_ref[1, rs, :] = pltpu.roll(ki_a, QUARTER, axis=1).astype(BF16)
    wit_ref[:, rs] = kiw.T[QUARTER:QUARTER + IDX_HEADS, :]


def _proj_kernel(*refs):
    for r in range(TM_PROJ // SUB_PROJ):
        _proj_rows(r * SUB_PROJ, SUB_PROJ, *refs)


def _proj(x2d, g, w_p, head_mean, qg, kg, cos, sin):
    t, tm = x2d.shape[0], TM_PROJ
    row = lambda i: (i, 0)
    colb = lambda i: (0, i)
    n_pairs = ATTN_WIDTH // LANES
    stack = lambda n: (jax.ShapeDtypeStruct((n, t, LANES), BF16),
                       pl.BlockSpec((n, tm, LANES), lambda i: (0, i, 0)))
    (q_shape, q_spec), (k_shape, k_spec) = stack(n_pairs), stack(2)
    return pl.pallas_call(
        _proj_kernel,
        grid=(t // tm,),
        in_specs=[
            pl.BlockSpec((tm, D_MODEL), row),
            _const_spec((1, D_MODEL)),
            _const_spec((D_MODEL, C_END)),
            _const_spec((ATTN_WIDTH, ATTN_WIDTH)),
            _const_spec((1, ATTN_WIDTH)),
            _const_spec((1, LANES)),
            pl.BlockSpec((tm, QUARTER), row),
            pl.BlockSpec((tm, QUARTER), row),
        ],
        out_specs=[q_spec, q_spec, k_spec, pl.BlockSpec((LANES, tm), colb), k_spec,
                   pl.BlockSpec((IDX_HEADS, tm), colb)],
        out_shape=[q_shape, q_shape, k_shape, jax.ShapeDtypeStruct((LANES, t), BF16), k_shape,
                   jax.ShapeDtypeStruct((IDX_HEADS, t), F32)],
        compiler_params=pltpu.CompilerParams(
            dimension_semantics=("arbitrary",), vmem_limit_bytes=VMEM_LIMIT),
        name="proj",
    )(x2d, g, w_p, head_mean, qg, kg, cos, sin)


def _key_to_f32(key):
    return pltpu.bitcast(key ^ ((key >> 31) & 0x7FFFFFFF), F32)


def _colsum(x):
    n, tq = x.shape
    part = jnp.sum(x.reshape(n // 64, 64, tq), axis=0)
    return jnp.sum(part, axis=0, keepdims=True)


def _indexer_rows(ki_ref, qi_ref, wit_ref, hd, r0, rows):
    d = _dot_nt(ki_ref[hd & 1, pl.ds(r0, rows), :], qi_ref[hd >> 1])
    return wit_ref[pl.ds(hd, 1), :] * jnp.maximum(d, 0.0)


def _attn_step(lmax_ref, q_ref, qi_ref, wit_ref, qin_ref, witn_ref, k_ref, vt_ref, ki_ref,
               y_ref, sc_s, lg_s, bias_s, *, j, nq, topk, idx_scale):
    tq = q_ref.shape[1]
    sk = (j + 1) * tq
    cur, nxt = j % 2, (j + 1) % 2
    has_next = j + 1 < nq
    skn = (j + 2) * tq
    qidx = j * tq + lax.broadcasted_iota(jnp.int32, (1, tq), 1)
    kidx = lax.broadcasted_iota(jnp.int32, (sk, tq), 0)

    if j == 0:
        raw = jnp.zeros((sk, tq), F32)
        for hd in range(IDX_HEADS):
            raw = raw + _indexer_rows(ki_ref, qi_ref, wit_ref, hd, 0, sk)
    else:
        raw = sc_s[cur, 0:sk, :]
    sc_s[cur, 0:sk, :] = jnp.where(kidx <= qidx, raw * idx_scale, -jnp.inf)
    if has_next:
        sc_s[nxt, 0:skn, :] = jnp.zeros((skn, tq), F32)

    kf = float(topk)

    def count_ge(cand):
        acc = jnp.zeros((CNT_ROWS, tq), F32)
        for r in range(0, sk, CNT_ROWS):
            acc = jnp.where(sc_s[cur, r:r + CNT_ROWS, :] >= cand, acc + 1.0, acc)
        return jnp.sum(acc, axis=0, keepdims=True)

    def search(i, t):
        for bit in (31 - 2 * i, 30 - 2 * i):
            cand = t + jnp.left_shift(jnp.int32(1), bit)
            t = jnp.where(count_ge(_key_to_f32(cand)) >= kf, cand, t)
        hd, part = i >> 1, i & 1
        r0 = pl.multiple_of(part * (sk // 2), LANES)
        lg_s[hd, pl.ds(r0, sk // 2), :] = _dot_nt(k_ref[hd & 1, pl.ds(r0, sk // 2), :],
                                                   q_ref[hd >> 1])
        if has_next:
            r0n = pl.multiple_of(part * (skn // 2), LANES)
            sc_s[nxt, pl.ds(r0n, skn // 2), :] += _indexer_rows(
                ki_ref, qin_ref, witn_ref, hd, r0n, skn // 2)
        return t

    tkey = lax.fori_loop(0, 2 * N_HEADS, search, jnp.full((1, tq), INT_MIN, jnp.int32))
    tkey = jnp.maximum(tkey, KEY_NEG_INF)
    thr = _key_to_f32(tkey)

    n_ge = count_ge(thr)
    tie = jnp.where(tkey > KEY_NEG_INF, jnp.where(n_ge > kf, 1.0, 0.0), 0.0)
    any_tie = jnp.max(tie) > 0.0

    def tie_cut():
        sc = sc_s[cur, 0:sk, :]
        need = kf - _colsum(jnp.where(sc > thr, 1.0, 0.0))
        eq = jnp.where(sc == thr, 1.0, 0.0)

        def step(i, c):
            cand = c + jnp.left_shift(jnp.int32(1), 11 - i)
            cnt = _colsum(jnp.where(kidx < cand, eq, 0.0))
            return jnp.where(cnt < need, cand, c)

        return lax.fori_loop(0, 12, step, jnp.zeros((1, tq), jnp.int32))

    cut = lax.cond(any_tie, tie_cut, lambda: jnp.full((1, tq), sk, jnp.int32))
    cut = jnp.minimum(cut, qidx)
    bounded = lmax_ref[0] <= MAX_SAFE_LOGIT
    shift = jnp.where(bounded, lmax_ref[0], 0.0)
    sc = sc_s[cur, 0:sk, :]
    bias_s[0:sk, :] = jnp.where(
        sc > thr, -shift, jnp.where(sc == thr, jnp.where(kidx <= cut, -shift, NEG_BIG), NEG_BIG))

    @pl.when(jnp.logical_not(bounded))
    def _():
        def per_head(hd, carry):
            def blk(ci):
                return pl.ds(pl.multiple_of(ci * tq, tq), tq)

            def row_max(ci, m):
                s = lg_s[hd, blk(ci), :] + bias_s[blk(ci), :]
                return jnp.maximum(m, jnp.max(s, axis=0, keepdims=True))

            m = lax.fori_loop(0, j + 1, row_max, jnp.full((1, tq), -jnp.inf, F32))

            def sub(ci, c):
                lg_s[hd, blk(ci), :] = lg_s[hd, blk(ci), :] - m
                return c

            return lax.fori_loop(0, j + 1, sub, carry)

        lax.fori_loop(0, N_HEADS, per_head, 0)

    vt = vt_ref[:, 0:sk]
    outs = []
    for hd in range(N_HEADS):
        s = lg_s[hd, 0:sk, :] + bias_s[0:sk, :]
        o = _dot(vt, jnp.exp2(s).astype(BF16))
        outs.append(o[0:HEAD_DIM, :] / o[HEAD_DIM:HEAD_DIM + 1, :])
    y_ref[...] = jnp.concatenate(outs, axis=0).T.astype(BF16)


def _attn_kernel(*refs, nq, topk, idx_scale):
    j = pl.program_id(1)
    for c in range(nq):
        pl.when(j == c)(functools.partial(
            _attn_step, *refs, j=c, nq=nq, topk=topk, idx_scale=idx_scale))


def _attn(logit_bound, q, qi, wit, k, vt, ki, *, batch, seq, topk):
    tq = TQ_ATTN
    assert IDX_HEADS == N_HEADS and seq % tq == 0 and tq % LANES == 0
    nq = seq // tq
    n_pairs = ATTN_WIDTH // LANES
    blk = lambda b, j: b * nq + j
    nblk = lambda b, j: b * nq + jnp.minimum(j + 1, nq - 1)
    qspec = lambda f: pl.BlockSpec((n_pairs, tq, LANES), lambda b, j: (0, f(b, j), 0))
    wspec = lambda f: pl.BlockSpec((IDX_HEADS, tq), lambda b, j: (0, f(b, j)))
    kspec = pl.BlockSpec((2, seq, LANES), lambda b, j: (0, b, 0))
    return pl.pallas_call(
        functools.partial(_attn_kernel, nq=nq, topk=topk,
                          idx_scale=(IDX_DIM ** -0.5) * (IDX_HEADS ** -0.5)),
        grid=(batch, nq),
        in_specs=[
            pl.BlockSpec(memory_space=pltpu.SMEM),
            qspec(blk), qspec(blk), wspec(blk), qspec(nblk), wspec(nblk),
            kspec,
            pl.BlockSpec((LANES, seq), lambda b, j: (0, b)),
            kspec,
        ],
        out_specs=pl.BlockSpec((tq, ATTN_WIDTH), lambda b, j: (blk(b, j), 0)),
        out_shape=jax.ShapeDtypeStruct((batch * seq, ATTN_WIDTH), BF16),
        scratch_shapes=[pltpu.VMEM((2, seq, tq), F32), pltpu.VMEM((N_HEADS, seq, tq), F32),
                        pltpu.VMEM((seq, tq), F32)],
        compiler_params=pltpu.CompilerParams(
            dimension_semantics=("arbitrary", "arbitrary"), vmem_limit_bytes=VMEM_LIMIT),
        name="attn",
    )(logit_bound, q, qi, wit, qi, wit, k, vt, ki)


def _merge_rows(r0, rows, t, x_ref, ya_ref, g_ref, wug_ref, pw_ref, pb_ref, ps_ref, pa_ref, pp_ref,
                wo_ref, o_ref, ubuf):
    rs = slice(r0, r0 + rows)
    x = x_ref[rs, :]
    h = _rms(x, g_ref[...]).astype(BF16)
    z = _dot(h, wug_ref[...])
    u = z[:, 0:POOL_WIDTH]
    g_attn = z[:, POOL_WIDTH:POOL_WIDTH + D_MODEL]
    g_pool = z[:, POOL_WIDTH + D_MODEL:POOL_WIDTH + 2 * D_MODEL]

    base = MAX_WIN + r0
    ubuf[base:base + rows, :] = u
    tm = x_ref.shape[0]
    pos1 = (t * tm + r0 + 1 + lax.broadcasted_iota(jnp.int32, (rows, 1), 0)).astype(F32)
    pb = pb_ref[...]
    ps = ps_ref[...]
    y_pool = []
    for gi, win in enumerate(POOL_WINDOWS):
        sl = slice(gi * POOL_GROUP_DIM, (gi + 1) * POOL_GROUP_DIM)
        ug = u[:, sl]
        acc = ug
        for s in range(1, win):
            acc = acc + ubuf[base - s:base - s + rows, sl]
        pooled = (acc / jnp.minimum(pos1, float(win)) - ug).astype(BF16)
        mixed = _dot(pooled, pw_ref[gi]) + pb[:, sl]
        y_pool.append((mixed * ps[:, sl]).astype(BF16))
    y_pool = jnp.concatenate(y_pool, axis=1)

    merged = (jax.nn.sigmoid(g_attn) * _dot(ya_ref[rs, :], pa_ref[...])
              + jax.nn.sigmoid(g_pool) * _dot(y_pool, pp_ref[...]))
    o_ref[rs, :] = x + _dot(merged.astype(BF16), wo_ref[...])


def _merge_kernel(*refs):
    ubuf = refs[-1]
    tm = refs[0].shape[0]
    t = pl.program_id(1)

    @pl.when(t == 0)
    def _():
        ubuf[0:MAX_WIN, :] = jnp.zeros((MAX_WIN, POOL_WIDTH), F32)

    for r in range(0, tm, SUB_MERGE):
        _merge_rows(r, SUB_MERGE, t, *refs)
    ubuf[0:MAX_WIN, :] = ubuf[tm:tm + MAX_WIN, :]


def _merge(x2d, y_attn, g, w_ug, pool_w, pool_b, pool_scale, proj_attn, proj_pool, w_out,
           *, batch, seq):
    tm = TM_MERGE
    nt = seq // tm
    row = lambda b, i: (b * nt + i, 0)
    return pl.pallas_call(
        _merge_kernel,
        grid=(batch, nt),
        in_specs=[
            pl.BlockSpec((tm, D_MODEL), row),
            pl.BlockSpec((tm, ATTN_WIDTH), row),
            _const_spec((1, D_MODEL)),
            _const_spec((D_MODEL, POOL_WIDTH + 2 * D_MODEL)),
            _const_spec((POOL_GROUPS, POOL_GROUP_DIM, POOL_GROUP_DIM)),
            _const_spec((1, POOL_WIDTH)),
            _const_spec((1, POOL_WIDTH)),
            _const_spec((ATTN_WIDTH, D_MODEL)),
            _const_spec((POOL_WIDTH, D_MODEL)),
            _const_spec((D_MODEL, D_MODEL)),
        ],
        out_specs=pl.BlockSpec((tm, D_MODEL), row),
        out_shape=jax.ShapeDtypeStruct((batch * seq, D_MODEL), F32),
        scratch_shapes=[pltpu.VMEM((MAX_WIN + tm, POOL_WIDTH), F32)],
        compiler_params=pltpu.CompilerParams(
            dimension_semantics=("arbitrary", "arbitrary"), vmem_limit_bytes=VMEM_LIMIT),
        name="merge",
    )(x2d, y_attn, g, w_ug, pool_w, pool_b, pool_scale, proj_attn, proj_pool, w_out)


def _rope_tables(positions):
    inv_freq = ROPE_THETA ** (-jnp.arange(0, HEAD_DIM, 2, dtype=F32) / HEAD_DIM)
    per_row = LANES // QUARTER
    ang = (positions.astype(F32).reshape(-1, per_row, 1) * inv_freq).reshape(-1, LANES)
    cos, sin = lax.optimization_barrier((jnp.cos(ang), jnp.sin(ang)))
    return cos.reshape(-1, QUARTER), sin.reshape(-1, QUARTER)


def kernel(x, positions, ffn1_norm, ffn1_w1, ffn1_w3, ffn1_w2, mix_norm, w_in, q_norm, k_norm,
           pool_w, pool_b, pool_scale, proj_attn, proj_pool, w_out,
           ffn2_norm, ffn2_w1, ffn2_w3, ffn2_w2):
    batch, seq, _ = x.shape
    depth = ffn1_norm.shape[0]
    topk = min(TOPK_MAX, seq // 4)
    cos, sin = _rope_tables(positions)
    head_of_lane = _quarter_interleave(np.repeat(np.arange(N_HEADS), HEAD_DIM))
    head_mean = jnp.asarray(
        (head_of_lane[:, None] == head_of_lane[None, :]) * (1.0 / HEAD_DIM), BF16)
    bf = lambda w: w.astype(BF16)

    xt = x.reshape(batch * seq, D_MODEL)
    for l in range(depth):
        w_proj = _quarter_interleave(bf(w_in[l, :, 0:C_END]))
        w_ug = bf(w_in[l, :, C_U:])
        qg = _quarter_interleave(jnp.tile(q_norm[l], N_HEADS))[None]
        kg = _quarter_interleave(jnp.concatenate([k_norm[l], jnp.ones_like(k_norm[l])]))[None]
        xt = _ffn(xt, ffn1_norm[l][None], bf(ffn1_w1[l]), bf(ffn1_w3[l]), bf(ffn1_w2[l]))
        q, qi, k, vt, ki, wit = _proj(xt, mix_norm[l][None], w_proj, head_mean, qg, kg, cos, sin)
        logit_bound = (BOUND_MARGIN * HEAD_DIM * Q_SCALE * jnp.max(jnp.abs(q_norm[l]))
                       * jnp.max(jnp.abs(k_norm[l]))).reshape(1)
        y_attn = _attn(logit_bound, q, qi, wit, k, vt, ki, batch=batch, seq=seq, topk=topk)
        xt = _merge(xt, y_attn, mix_norm[l][None], w_ug, bf(pool_w[l]),
                    pool_b[l].reshape(1, POOL_WIDTH), pool_scale[l][None],
                    bf(proj_attn[l]), bf(proj_pool[l]), bf(w_out[l]), batch=batch, seq=seq)
        xt = _ffn(xt, ffn2_norm[l][None], bf(ffn2_w1[l]), bf(ffn2_w3[l]), bf(ffn2_w2[l]))
    return xt.reshape(batch, seq, D_MODEL)
```

```python
import functools
import math

import jax
import jax.numpy as jnp
import numpy as np
from jax import lax
from jax.experimental import pallas as pl
from jax.experimental.pallas import tpu as pltpu

D_MODEL = 1024
N_HEADS = 8
HEAD_DIM = 64
ATTN_WIDTH = N_HEADS * HEAD_DIM
IDX_HEADS = 8
IDX_DIM = 64
TOPK_MAX = 256
POOL_WINDOWS = (2, 4, 8, 16)
POOL_GROUPS = len(POOL_WINDOWS)
POOL_GROUP_DIM = 128
POOL_WIDTH = POOL_GROUPS * POOL_GROUP_DIM
D_FF = 2816
ROPE_THETA = 10000.0
EPS = 1e-6
SPLITS = (ATTN_WIDTH, HEAD_DIM, HEAD_DIM, IDX_HEADS * IDX_DIM, IDX_DIM, IDX_HEADS,
          POOL_WIDTH, D_MODEL, D_MODEL)

LANES = 128
SUBLANES = 8
MAX_WIN = max(POOL_WINDOWS)
INT_MIN = -(2 ** 31)
KEY_NEG_INF = INT_MIN + 0x7FFFFF
NEG_BIG = -1e30
Q_SCALE = (HEAD_DIM ** -0.5) * math.log2(math.e)
BOUND_MARGIN = 1.02
MAX_SAFE_LOGIT = 60.0
VMEM_LIMIT = 56 * 1024 * 1024

TM_FFN = 512
SUB_FFN = 256
TM_PROJ = 512
SUB_PROJ = 128
TM_MERGE = 512
SUB_MERGE = 256
TQ_ATTN = 256
CNT_ROWS = 64

F32 = jnp.float32
BF16 = jnp.bfloat16

_PTS = [int(c) for c in np.cumsum((0,) + SPLITS)]
C_Q, C_KV, C_QI, C_KIW, C_U = _PTS[0], _PTS[1], _PTS[3], _PTS[4], _PTS[6]
C_WI_LANE = _PTS[5] - C_KIW
C_END = C_KIW + LANES
assert (C_KV % LANES, C_QI % LANES, C_KIW % LANES) == (0, 0, 0)
assert C_QI - C_KV == 2 * HEAD_DIM == LANES and C_WI_LANE == IDX_DIM


def _const_spec(shape):
    return pl.BlockSpec(shape, lambda *_: (0,) * len(shape), pipeline_mode=pl.Buffered(1))


def _rms(x, g):
    ms = jnp.mean(x * x, axis=-1, keepdims=True)
    return (x * lax.rsqrt(ms + EPS)) * g


def _dot(a, b):
    return jnp.dot(a, b, preferred_element_type=F32)


def _ffn_kernel(x_ref, g_ref, w1_ref, w3_ref, w2_ref, o_ref):
    for r in range(0, TM_FFN, SUB_FFN):
        x = x_ref[r:r + SUB_FFN, :]
        h = _rms(x, g_ref[...]).astype(BF16)
        a = _dot(h, w1_ref[...])
        b = _dot(h, w3_ref[...])
        act = ((a * jax.nn.sigmoid(a)) * b).astype(BF16)
        o_ref[r:r + SUB_FFN, :] = x + 0.5 * _dot(act, w2_ref[...])


def _ffn(x2d, g, w1, w3, w2):
    t, tm = x2d.shape[0], TM_FFN
    return pl.pallas_call(
        _ffn_kernel,
        grid=(t // tm,),
        in_specs=[
            pl.BlockSpec((tm, D_MODEL), lambda i: (i, 0)),
            _const_spec((1, D_MODEL)),
            _const_spec((D_MODEL, D_FF)),
            _const_spec((D_MODEL, D_FF)),
            _const_spec((D_FF, D_MODEL)),
        ],
        out_specs=pl.BlockSpec((tm, D_MODEL), lambda i: (i, 0)),
        out_shape=jax.ShapeDtypeStruct((t, D_MODEL), F32),
        compiler_params=pltpu.CompilerParams(
            dimension_semantics=("arbitrary",), vmem_limit_bytes=VMEM_LIMIT),
        name="ffn",
    )(x2d, g, w1, w3, w2)


QUARTER = HEAD_DIM // 2


def _quarter_interleave(a, axis=-1):
    axis = axis % a.ndim
    shp = a.shape
    n = shp[axis] // LANES
    a = a.reshape(shp[:axis] + (n, 2, 2, QUARTER) + shp[axis + 1:])
    return jnp.swapaxes(a, axis + 1, axis + 2).reshape(shp)


def _proj_rows(r0, rows, x_ref, g_ref, w_ref, hm_ref, qg_ref, kg_ref, cos_ref, sin_ref,
               qt_ref, qit_ref, k_ref, vt_ref, ki_ref, wit_ref):
    rs = slice(r0, r0 + rows)
    h = _rms(x_ref[rs, :], g_ref[...]).astype(BF16)
    z = _dot(h, w_ref[...])
    c32 = cos_ref[rs, :]
    s32 = sin_ref[rs, :]
    cos = jnp.concatenate([c32, c32, c32, c32], axis=1)
    sin = jnp.concatenate([-s32, -s32, s32, s32], axis=1)
    lane = lax.broadcasted_iota(jnp.int32, (rows, LANES), 1)
    is_a = (lane & QUARTER) == 0

    def rope(x):
        return x * cos + pltpu.roll(x, HEAD_DIM, axis=1) * sin

    q = z[:, C_Q:C_Q + ATTN_WIDTH]
    q_ms = _dot((q * q).astype(BF16), hm_ref[...])
    qn = (q * lax.rsqrt(q_ms + EPS)) * qg_ref[...]
    for s in range(ATTN_WIDTH // LANES):
        sl = slice(s * LANES, (s + 1) * LANES)
        qt_ref[s, :, rs] = (rope(qn[:, sl]) * Q_SCALE).T.astype(BF16)
        qit_ref[s, :, rs] = rope(z[:, C_QI + s * LANES:C_QI + (s + 1) * LANES]).T.astype(BF16)

    kv = z[:, C_KV:C_KV + LANES]
    ms = jnp.sum(jnp.where(is_a, kv * kv, 0.0), axis=-1, keepdims=True) * (1.0 / HEAD_DIM)
    k_a = jnp.where(is_a, rope((kv * lax.rsqrt(ms + EPS)) * kg_ref[...]), 0.0)
    k_ref[0, rs, :] = k_a.astype(BF16)
    k_ref[1, rs, :] = pltpu.roll(k_a, QUARTER, axis=1).astype(BF16)

    v = jnp.where(lane < QUARTER, pltpu.roll(kv, LANES - QUARTER, axis=1),
                  pltpu.roll(kv, HEAD_DIM, axis=1))
    vt_ref[:, rs] = jnp.where(lane < HEAD_DIM, v, 1.0).T.astype(BF16)

    kiw = z[:, C_KIW:C_KIW + LANES]
    ki_a = jnp.where(is_a, rope(kiw), 0.0)
    ki_ref[0, rs, :] = ki_a.astype(BF16)
    ki_ref[1, rs, :] = pltpu.roll(ki_a, QUARTER, axis=1).astype(BF16)
    wit_ref[:, rs] = kiw.T[QUARTER:QUARTER + IDX_HEADS, :]


def _proj_kernel(*refs):
    for r in range(TM_PROJ // SUB_PROJ):
        _proj_rows(r * SUB_PROJ, SUB_PROJ, *refs)


def _proj(x2d, g, w_p, head_mean, qg, kg, cos, sin):
    t, tm = x2d.shape[0], TM_PROJ
    row = lambda i: (i, 0)
    colb = lambda i: (0, i)
    n_pairs = ATTN_WIDTH // LANES
    stack = lambda n: (jax.ShapeDtypeStruct((n, t, LANES), BF16),
                       pl.BlockSpec((n, tm, LANES), lambda i: (0, i, 0)))
    k_shape, k_spec = stack(2)
    q_shape = jax.ShapeDtypeStruct((n_pairs, LANES, t), BF16)
    q_spec = pl.BlockSpec((n_pairs, LANES, tm), lambda i: (0, 0, i))
    return pl.pallas_call(
        _proj_kernel,
        grid=(t // tm,),
        in_specs=[
            pl.BlockSpec((tm, D_MODEL), row),
            _const_spec((1, D_MODEL)),
            _const_spec((D_MODEL, C_END)),
            _const_spec((ATTN_WIDTH, ATTN_WIDTH)),
            _const_spec((1, ATTN_WIDTH)),
            _const_spec((1, LANES)),
            pl.BlockSpec((tm, QUARTER), row),
            pl.BlockSpec((tm, QUARTER), row),
        ],
        out_specs=[q_spec, q_spec, k_spec, pl.BlockSpec((LANES, tm), colb), k_spec,
                   pl.BlockSpec((IDX_HEADS, tm), colb)],
        out_shape=[q_shape, q_shape, k_shape, jax.ShapeDtypeStruct((LANES, t), BF16), k_shape,
                   jax.ShapeDtypeStruct((IDX_HEADS, t), F32)],
        compiler_params=pltpu.CompilerParams(
            dimension_semantics=("arbitrary",), vmem_limit_bytes=VMEM_LIMIT),
        name="proj",
    )(x2d, g, w_p, head_mean, qg, kg, cos, sin)


def _key_to_f32(key):
    return pltpu.bitcast(key ^ ((key >> 31) & 0x7FFFFFFF), F32)


def _colsum(x):
    n, tq = x.shape
    part = jnp.sum(x.reshape(n // 64, 64, tq), axis=0)
    return jnp.sum(part, axis=0, keepdims=True)


def _indexer_rows(ki_ref, qit_ref, wit_ref, hd, r0, rows):
    d = _dot(ki_ref[hd & 1, pl.ds(r0, rows), :], qit_ref[hd >> 1])
    return wit_ref[pl.ds(hd, 1), :] * jnp.maximum(d, 0.0)


def _attn_step(lmax_ref, qt_ref, qit_ref, wit_ref, qitn_ref, witn_ref, k_ref, vt_ref, ki_ref,
               y_ref, sc_s, lg_s, bias_s, *, j, nq, topk, idx_scale):
    tq = qt_ref.shape[2]
    sk = (j + 1) * tq
    cur, nxt = j % 2, (j + 1) % 2
    has_next = j + 1 < nq
    skn = (j + 2) * tq
    qidx = j * tq + lax.broadcasted_iota(jnp.int32, (1, tq), 1)
    kidx = lax.broadcasted_iota(jnp.int32, (sk, tq), 0)

    if j == 0:
        raw = jnp.zeros((sk, tq), F32)
        for hd in range(IDX_HEADS):
            raw = raw + _indexer_rows(ki_ref, qit_ref, wit_ref, hd, 0, sk)
    else:
        raw = sc_s[cur, 0:sk, :]
    sc_s[cur, 0:sk, :] = jnp.where(kidx <= qidx, raw * idx_scale, -jnp.inf)
    if has_next:
        sc_s[nxt, 0:skn, :] = jnp.zeros((skn, tq), F32)

    kf = float(topk)

    def count_ge(cand):
        acc = jnp.zeros((CNT_ROWS, tq), F32)
        for r in range(0, sk, CNT_ROWS):
            acc = jnp.where(sc_s[cur, r:r + CNT_ROWS, :] >= cand, acc + 1.0, acc)
        return jnp.sum(acc, axis=0, keepdims=True)

    def search(i, t):
        for bit in (31 - 2 * i, 30 - 2 * i):
            cand = t + jnp.left_shift(jnp.int32(1), bit)
            t = jnp.where(count_ge(_key_to_f32(cand)) >= kf, cand, t)
        hd, part = i >> 1, i & 1
        r0 = pl.multiple_of(part * (sk // 2), LANES)
        lg_s[hd, pl.ds(r0, sk // 2), :] = _dot(k_ref[hd & 1, pl.ds(r0, sk // 2), :], qt_ref[hd >> 1])
        if has_next:
            r0n = pl.multiple_of(part * (skn // 2), LANES)
            sc_s[nxt, pl.ds(r0n, skn // 2), :] += _indexer_rows(
                ki_ref, qitn_ref, witn_ref, hd, r0n, skn // 2)
        return t

    tkey = lax.fori_loop(0, 2 * N_HEADS, search, jnp.full((1, tq), INT_MIN, jnp.int32))
    tkey = jnp.maximum(tkey, KEY_NEG_INF)
    thr = _key_to_f32(tkey)

    n_ge = count_ge(thr)
    tie = jnp.where(tkey > KEY_NEG_INF, jnp.where(n_ge > kf, 1.0, 0.0), 0.0)
    any_tie = jnp.max(tie) > 0.0

    def tie_cut():
        sc = sc_s[cur, 0:sk, :]
        need = kf - _colsum(jnp.where(sc > thr, 1.0, 0.0))
        eq = jnp.where(sc == thr, 1.0, 0.0)

        def step(i, c):
            cand = c + jnp.left_shift(jnp.int32(1), 11 - i)
            cnt = _colsum(jnp.where(kidx < cand, eq, 0.0))
            return jnp.where(cnt < need, cand, c)

        return lax.fori_loop(0, 12, step, jnp.zeros((1, tq), jnp.int32))

    cut = lax.cond(any_tie, tie_cut, lambda: jnp.full((1, tq), sk, jnp.int32))
    cut = jnp.minimum(cut, qidx)
    bounded = lmax_ref[0] <= MAX_SAFE_LOGIT
    shift = jnp.where(bounded, lmax_ref[0], 0.0)
    sc = sc_s[cur, 0:sk, :]
    bias_s[0:sk, :] = jnp.where(
        sc > thr, -shift, jnp.where(sc == thr, jnp.where(kidx <= cut, -shift, NEG_BIG), NEG_BIG))

    @pl.when(jnp.logical_not(bounded))
    def _():
        def per_head(hd, carry):
            def blk(ci):
                return pl.ds(pl.multiple_of(ci * tq, tq), tq)

            def row_max(ci, m):
                s = lg_s[hd, blk(ci), :] + bias_s[blk(ci), :]
                return jnp.maximum(m, jnp.max(s, axis=0, keepdims=True))

            m = lax.fori_loop(0, j + 1, row_max, jnp.full((1, tq), -jnp.inf, F32))

            def sub(ci, c):
                lg_s[hd, blk(ci), :] = lg_s[hd, blk(ci), :] - m
                return c

            return lax.fori_loop(0, j + 1, sub, carry)

        lax.fori_loop(0, N_HEADS, per_head, 0)

    vt = vt_ref[:, 0:sk]
    outs = []
    for hd in range(N_HEADS):
        s = lg_s[hd, 0:sk, :] + bias_s[0:sk, :]
        o = _dot(vt, jnp.exp2(s).astype(BF16))
        outs.append(o[0:HEAD_DIM, :] / o[HEAD_DIM:HEAD_DIM + 1, :])
    y_ref[...] = jnp.concatenate(outs, axis=0).T.astype(BF16)


def _attn_kernel(*refs, nq, topk, idx_scale):
    j = pl.program_id(1)
    for c in range(nq):
        pl.when(j == c)(functools.partial(
            _attn_step, *refs, j=c, nq=nq, topk=topk, idx_scale=idx_scale))


def _attn(logit_bound, q, qi, wit, k, vt, ki, *, batch, seq, topk):
    tq = TQ_ATTN
    assert IDX_HEADS == N_HEADS and seq % tq == 0 and tq % LANES == 0
    nq = seq // tq
    n_pairs = ATTN_WIDTH // LANES
    blk = lambda b, j: b * nq + j
    nblk = lambda b, j: b * nq + jnp.minimum(j + 1, nq - 1)
    qspec = lambda f: pl.BlockSpec((n_pairs, LANES, tq), lambda b, j: (0, 0, f(b, j)))
    wspec = lambda f: pl.BlockSpec((IDX_HEADS, tq), lambda b, j: (0, f(b, j)))
    kspec = pl.BlockSpec((2, seq, LANES), lambda b, j: (0, b, 0))
    return pl.pallas_call(
        functools.partial(_attn_kernel, nq=nq, topk=topk,
                          idx_scale=(IDX_DIM ** -0.5) * (IDX_HEADS ** -0.5)),
        grid=(batch, nq),
        in_specs=[
            pl.BlockSpec(memory_space=pltpu.SMEM),
            qspec(blk), qspec(blk), wspec(blk), qspec(nblk), wspec(nblk),
            kspec,
            pl.BlockSpec((LANES, seq), lambda b, j: (0, b)),
            kspec,
        ],
        out_specs=pl.BlockSpec((tq, ATTN_WIDTH), lambda b, j: (blk(b, j), 0)),
        out_shape=jax.ShapeDtypeStruct((batch * seq, ATTN_WIDTH), BF16),
        scratch_shapes=[pltpu.VMEM((2, seq, tq), F32), pltpu.VMEM((N_HEADS, seq, tq), F32),
                        pltpu.VMEM((seq, tq), F32)],
        compiler_params=pltpu.CompilerParams(
            dimension_semantics=("arbitrary", "arbitrary"), vmem_limit_bytes=VMEM_LIMIT),
        name="attn",
    )(logit_bound, q, qi, wit, qi, wit, k, vt, ki)


def _merge_rows(r0, rows, t, x_ref, ya_ref, g_ref, wug_ref, pw_ref, pb_ref, ps_ref, pa_ref, pp_ref,
                wo_ref, o_ref, ubuf):
    rs = slice(r0, r0 + rows)
    x = x_ref[rs, :]
    h = _rms(x, g_ref[...]).astype(BF16)
    z = _dot(h, wug_ref[...])
    u = z[:, 0:POOL_WIDTH]
    g_attn = z[:, POOL_WIDTH:POOL_WIDTH + D_MODEL]
    g_pool = z[:, POOL_WIDTH + D_MODEL:POOL_WIDTH + 2 * D_MODEL]

    base = MAX_WIN + r0
    ubuf[base:base + rows, :] = u
    tm = x_ref.shape[0]
    pos1 = (t * tm + r0 + 1 + lax.broadcasted_iota(jnp.int32, (rows, 1), 0)).astype(F32)
    pb = pb_ref[...]
    ps = ps_ref[...]
    y_pool = []
    for gi, win in enumerate(POOL_WINDOWS):
        sl = slice(gi * POOL_GROUP_DIM, (gi + 1) * POOL_GROUP_DIM)
        ug = u[:, sl]
        acc = ug
        for s in range(1, win):
            acc = acc + ubuf[base - s:base - s + rows, sl]
        pooled = (acc / jnp.minimum(pos1, float(win)) - ug).astype(BF16)
        mixed = _dot(pooled, pw_ref[gi]) + pb[:, sl]
        y_pool.append((mixed * ps[:, sl]).astype(BF16))
    y_pool = jnp.concatenate(y_pool, axis=1)

    merged = (jax.nn.sigmoid(g_attn) * _dot(ya_ref[rs, :], pa_ref[...])
              + jax.nn.sigmoid(g_pool) * _dot(y_pool, pp_ref[...]))
    o_ref[rs, :] = x + _dot(merged.astype(BF16), wo_ref[...])


def _merge_kernel(*refs):
    ubuf = refs[-1]
    tm = refs[0].shape[0]
    t = pl.program_id(1)

    @pl.when(t == 0)
    def _():
        ubuf[0:MAX_WIN, :] = jnp.zeros((MAX_WIN, POOL_WIDTH), F32)

    for r in range(0, tm, SUB_MERGE):
        _merge_rows(r, SUB_MERGE, t, *refs)
    ubuf[0:MAX_WIN, :] = ubuf[tm:tm + MAX_WIN, :]


def _merge(x2d, y_attn, g, w_ug, pool_w, pool_b, pool_scale, proj_attn, proj_pool, w_out,
           *, batch, seq):
    tm = TM_MERGE
    nt = seq // tm
    row = lambda b, i: (b * nt + i, 0)
    return pl.pallas_call(
        _merge_kernel,
        grid=(batch, nt),
        in_specs=[
            pl.BlockSpec((tm, D_MODEL), row),
            pl.BlockSpec((tm, ATTN_WIDTH), row),
            _const_spec((1, D_MODEL)),
            _const_spec((D_MODEL, POOL_WIDTH + 2 * D_MODEL)),
            _const_spec((POOL_GROUPS, POOL_GROUP_DIM, POOL_GROUP_DIM)),
            _const_spec((1, POOL_WIDTH)),
            _const_spec((1, POOL_WIDTH)),
            _const_spec((ATTN_WIDTH, D_MODEL)),
            _const_spec((POOL_WIDTH, D_MODEL)),
            _const_spec((D_MODEL, D_MODEL)),
        ],
        out_specs=pl.BlockSpec((tm, D_MODEL), row),
        out_shape=jax.ShapeDtypeStruct((batch * seq, D_MODEL), F32),
        scratch_shapes=[pltpu.VMEM((MAX_WIN + tm, POOL_WIDTH), F32)],
        compiler_params=pltpu.CompilerParams(
            dimension_semantics=("arbitrary", "arbitrary"), vmem_limit_bytes=VMEM_LIMIT),
        name="merge",
    )(x2d, y_attn, g, w_ug, pool_w, pool_b, pool_scale, proj_attn, proj_pool, w_out)


def _rope_tables(positions):
    inv_freq = ROPE_THETA ** (-jnp.arange(0, HEAD_DIM, 2, dtype=F32) / HEAD_DIM)
    per_row = LANES // QUARTER
    ang = (positions.astype(F32).reshape(-1, per_row, 1) * inv_freq).reshape(-1, LANES)
    cos, sin = lax.optimization_barrier((jnp.cos(ang), jnp.sin(ang)))
    return cos.reshape(-1, QUARTER), sin.reshape(-1, QUARTER)


def kernel(x, positions, ffn1_norm, ffn1_w1, ffn1_w3, ffn1_w2, mix_norm, w_in, q_norm, k_norm,
           pool_w, pool_b, pool_scale, proj_attn, proj_pool, w_out,
           ffn2_norm, ffn2_w1, ffn2_w3, ffn2_w2):
    batch, seq, _ = x.shape
    depth = ffn1_norm.shape[0]
    topk = min(TOPK_MAX, seq // 4)
    cos, sin = _rope_tables(positions)
    head_of_lane = _quarter_interleave(np.repeat(np.arange(N_HEADS), HEAD_DIM))
    head_mean = jnp.asarray(
        (head_of_lane[:, None] == head_of_lane[None, :]) * (1.0 / HEAD_DIM), BF16)
    bf = lambda w: w.astype(BF16)

    xt = x.reshape(batch * seq, D_MODEL)
    for l in range(depth):
        w_proj = _quarter_interleave(bf(w_in[l, :, 0:C_END]))
        w_ug = bf(w_in[l, :, C_U:])
        qg = _quarter_interleave(jnp.tile(q_norm[l], N_HEADS))[None]
        kg = _quarter_interleave(jnp.concatenate([k_norm[l], jnp.ones_like(k_norm[l])]))[None]
        xt = _ffn(xt, ffn1_norm[l][None], bf(ffn1_w1[l]), bf(ffn1_w3[l]), bf(ffn1_w2[l]))
        q, qi, k, vt, ki, wit = _proj(xt, mix_norm[l][None], w_proj, head_mean, qg, kg, cos, sin)
        logit_bound = (BOUND_MARGIN * HEAD_DIM * Q_SCALE * jnp.max(jnp.abs(q_norm[l]))
                       * jnp.max(jnp.abs(k_norm[l]))).reshape(1)
        y_attn = _attn(logit_bound, q, qi, wit, k, vt, ki, batch=batch, seq=seq, topk=topk)
        xt = _merge(xt, y_attn, mix_norm[l][None], w_ug, bf(pool_w[l]),
                    pool_b[l].reshape(1, POOL_WIDTH), pool_scale[l][None],
                    bf(proj_attn[l]), bf(proj_pool[l]), bf(w_out[l]), batch=batch, seq=seq)
        xt = _ffn(xt, ffn2_norm[l][None], bf(ffn2_w1[l]), bf(ffn2_w3[l]), bf(ffn2_w2[l]))
    return xt.reshape(batch, seq, D_MODEL)
```

```python
import functools
import math

import jax
import jax.numpy as jnp
import numpy as np
from jax import lax
from jax.experimental import pallas as pl
from jax.experimental.pallas import tpu as pltpu

D_MODEL = 1024
N_HEADS = 8
HEAD_DIM = 64
ATTN_WIDTH = N_HEADS * HEAD_DIM
IDX_HEADS = 8
IDX_DIM = 64
TOPK_MAX = 256
POOL_WINDOWS = (2, 4, 8, 16)
POOL_GROUPS = len(POOL_WINDOWS)
POOL_GROUP_DIM = 128
POOL_WIDTH = POOL_GROUPS * POOL_GROUP_DIM
D_FF = 2816
ROPE_THETA = 10000.0
EPS = 1e-6
SPLITS = (ATTN_WIDTH, HEAD_DIM, HEAD_DIM, IDX_HEADS * IDX_DIM, IDX_DIM, IDX_HEADS,
          POOL_WIDTH, D_MODEL, D_MODEL)

LANES = 128
SUBLANES = 8
MAX_WIN = max(POOL_WINDOWS)
INT_MIN = -(2 ** 31)
KEY_NEG_INF = INT_MIN + 0x7FFFFF
BF16_STEP = 1 << 16
NEG_BIG = -1e30
Q_SCALE = (HEAD_DIM ** -0.5) * math.log2(math.e)
BOUND_MARGIN = 1.02
MAX_SAFE_LOGIT = 60.0
VMEM_LIMIT = 56 * 1024 * 1024

TM_FFN = 512
SUB_FFN = 256
TM_PROJ = 512
SUB_PROJ = 128
TM_MERGE = 512
SUB_MERGE = 256
TQ_ATTN = 256
CNT_ROWS = 64

F32 = jnp.float32
BF16 = jnp.bfloat16

_PTS = [int(c) for c in np.cumsum((0,) + SPLITS)]
C_Q, C_KV, C_QI, C_KIW, C_U = _PTS[0], _PTS[1], _PTS[3], _PTS[4], _PTS[6]
C_WI_LANE = _PTS[5] - C_KIW
C_END = C_KIW + LANES
assert (C_KV % LANES, C_QI % LANES, C_KIW % LANES) == (0, 0, 0)
assert C_QI - C_KV == 2 * HEAD_DIM == LANES and C_WI_LANE == IDX_DIM


def _const_spec(shape):
    return pl.BlockSpec(shape, lambda *_: (0,) * len(shape), pipeline_mode=pl.Buffered(1))


def _rms(x, g):
    ms = jnp.mean(x * x, axis=-1, keepdims=True)
    return (x * lax.rsqrt(ms + EPS)) * g


def _dot(a, b):
    return jnp.dot(a, b, preferred_element_type=F32)


def _ffn_kernel(x_ref, g_ref, w1_ref, w3_ref, w2_ref, o_ref):
    for r in range(0, TM_FFN, SUB_FFN):
        x = x_ref[r:r + SUB_FFN, :]
        h = _rms(x, g_ref[...]).astype(BF16)
        a = _dot(h, w1_ref[...])
        b = _dot(h, w3_ref[...])
        act = ((a * jax.nn.sigmoid(a)) * b).astype(BF16)
        o_ref[r:r + SUB_FFN, :] = x + 0.5 * _dot(act, w2_ref[...])


def _ffn(x2d, g, w1, w3, w2):
    t, tm = x2d.shape[0], TM_FFN
    return pl.pallas_call(
        _ffn_kernel,
        grid=(t // tm,),
        in_specs=[
            pl.BlockSpec((tm, D_MODEL), lambda i: (i, 0)),
            _const_spec((1, D_MODEL)),
            _const_spec((D_MODEL, D_FF)),
            _const_spec((D_MODEL, D_FF)),
            _const_spec((D_FF, D_MODEL)),
        ],
        out_specs=pl.BlockSpec((tm, D_MODEL), lambda i: (i, 0)),
        out_shape=jax.ShapeDtypeStruct((t, D_MODEL), F32),
        compiler_params=pltpu.CompilerParams(
            dimension_semantics=("arbitrary",), vmem_limit_bytes=VMEM_LIMIT),
        name="ffn",
    )(x2d, g, w1, w3, w2)


QUARTER = HEAD_DIM // 2


def _quarter_interleave(a, axis=-1):
    axis = axis % a.ndim
    shp = a.shape
    n = shp[axis] // LANES
    a = a.reshape(shp[:axis] + (n, 2, 2, QUARTER) + shp[axis + 1:])
    return jnp.swapaxes(a, axis + 1, axis + 2).reshape(shp)


def _proj_rows(r0, rows, x_ref, g_ref, w_ref, hm_ref, qg_ref, kg_ref, cos_ref, sin_ref,
               q_ref, qi_ref, k_ref, vt_ref, ki_ref, wit_ref):
    rs = slice(r0, r0 + rows)
    h = _rms(x_ref[rs, :], g_ref[...]).astype(BF16)
    z = _dot(h, w_ref[...])
    c32 = cos_ref[rs, :]
    s32 = sin_ref[rs, :]
    cos = jnp.concatenate([c32, c32, c32, c32], axis=1)
    sin = jnp.concatenate([-s32, -s32, s32, s32], axis=1)
    lane = lax.broadcasted_iota(jnp.int32, (rows, LANES), 1)
    is_a = (lane & QUARTER) == 0

    def rope(x):
        return x * cos + pltpu.roll(x, HEAD_DIM, axis=1) * sin

    q = z[:, C_Q:C_Q + ATTN_WIDTH]
    q_ms = _dot((q * q).astype(BF16), hm_ref[...])
    qn = (q * lax.rsqrt(q_ms + EPS)) * qg_ref[...]
    for s in range(ATTN_WIDTH // LANES):
        sl = slice(s * LANES, (s + 1) * LANES)
        q_ref[s, rs, :] = (rope(qn[:, sl]) * Q_SCALE).astype(BF16)
        qi_ref[s, rs, :] = rope(z[:, C_QI + s * LANES:C_QI + (s + 1) * LANES]).astype(BF16)

    kv = z[:, C_KV:C_KV + LANES]
    ms = jnp.sum(jnp.where(is_a, kv * kv, 0.0), axis=-1, keepdims=True) * (1.0 / HEAD_DIM)
    k_a = jnp.where(is_a, rope((kv * lax.rsqrt(ms + EPS)) * kg_ref[...]), 0.0)
    k_ref[0, rs, :] = k_a.astype(BF16)
    k_ref[1, rs, :] = pltpu.roll(k_a, QUARTER, axis=1).astype(BF16)

    v = jnp.where(lane < QUARTER, pltpu.roll(kv, LANES - QUARTER, axis=1),
                  pltpu.roll(kv, HEAD_DIM, axis=1))
    vt_ref[:, rs] = jnp.where(lane < HEAD_DIM, v, 1.0).T.astype(BF16)

    kiw = z[:, C_KIW:C_KIW + LANES]
    ki_a = jnp.where(is_a, rope(kiw), 0.0)
    ki_ref[0, rs, :] = ki_a.astype(BF16)
    ki_ref[1, rs, :] = pltpu.roll(ki_a, QUARTER, axis=1).astype(BF16)
    wit_ref[:, rs] = kiw.T[QUARTER:QUARTER + IDX_HEADS, :]


def _proj_kernel(*refs):
    for r in range(TM_PROJ // SUB_PROJ):
        _proj_rows(r * SUB_PROJ, SUB_PROJ, *refs)


def _proj(x2d, g, w_p, head_mean, qg, kg, cos, sin):
    t, tm = x2d.shape[0], TM_PROJ
    row = lambda i: (i, 0)
    colb = lambda i: (0, i)
    n_pairs = ATTN_WIDTH // LANES
    stack = lambda n: (jax.ShapeDtypeStruct((n, t, LANES), BF16),
                       pl.BlockSpec((n, tm, LANES), lambda i: (0, i, 0)))
    (q_shape, q_spec), (k_shape, k_spec) = stack(n_pairs), stack(2)
    return pl.pallas_call(
        _proj_kernel,
        grid=(t // tm,),
        in_specs=[
            pl.BlockSpec((tm, D_MODEL), row),
            _const_spec((1, D_MODEL)),
            _const_spec((D_MODEL, C_END)),
            _const_spec((ATTN_WIDTH, ATTN_WIDTH)),
            _const_spec((1, ATTN_WIDTH)),
            _const_spec((1, LANES)),
            pl.BlockSpec((tm, QUARTER), row),
            pl.BlockSpec((tm, QUARTER), row),
        ],
        out_specs=[q_spec, q_spec, k_spec, pl.BlockSpec((LANES, tm), colb), k_spec,
                   pl.BlockSpec((IDX_HEADS, tm), colb)],
        out_shape=[q_shape, q_shape, k_shape, jax.ShapeDtypeStruct((LANES, t), BF16), k_shape,
                   jax.ShapeDtypeStruct((IDX_HEADS, t), F32)],
        compiler_params=pltpu.CompilerParams(
            dimension_semantics=("arbitrary",), vmem_limit_bytes=VMEM_LIMIT),
        name="proj",
    )(x2d, g, w_p, head_mean, qg, kg, cos, sin)


def _key_to_f32(key):
    return pltpu.bitcast(key ^ ((key >> 31) & 0x7FFFFFFF), F32)


def _colsum(x):
    n, tq = x.shape
    part = jnp.sum(x.reshape(n // 64, 64, tq), axis=0)
    return jnp.sum(part, axis=0, keepdims=True)


def _dot_nt(a, b):
    return lax.dot_general(a, b, (((1,), (1,)), ((), ())), preferred_element_type=F32)


def _indexer_rows(ki_ref, qi_ref, wit_ref, hd, r0, rows):
    d = _dot_nt(ki_ref[hd & 1, pl.ds(r0, rows), :], qi_ref[hd >> 1])
    return wit_ref[pl.ds(hd, 1), :] * jnp.maximum(d, 0.0)


def _attn_step(lmax_ref, q_ref, qi_ref, wit_ref, qin_ref, witn_ref, k_ref, vt_ref, ki_ref,
               y_ref, sc_a, sc_b, lg_s, bias_s, s16_s, *, j, nq, topk, idx_scale):
    tq = q_ref.shape[1]
    sk = (j + 1) * tq
    sc_cur, sc_nxt = (sc_a, sc_b) if j % 2 == 0 else (sc_b, sc_a)
    has_next = j + 1 < nq
    skn = (j + 2) * tq
    qidx = j * tq + lax.broadcasted_iota(jnp.int32, (1, tq), 1)
    kidx = lax.broadcasted_iota(jnp.int32, (sk, tq), 0)

    if j == 0:
        raw = jnp.zeros((sk, tq), F32)
        for hd in range(IDX_HEADS):
            raw = raw + _indexer_rows(ki_ref, qi_ref, wit_ref, hd, 0, sk)
    else:
        raw = sc_cur[0:sk, :]
    score = jnp.where(kidx <= qidx, raw * idx_scale, -jnp.inf)
    sc_cur[0:sk, :] = score
    s16_s[0:sk, :] = score.astype(BF16)
    if has_next:
        sc_nxt[0:skn, :] = jnp.zeros((skn, tq), F32)

    kf = float(topk)

    def count_ge(cand):
        acc = jnp.zeros((CNT_ROWS, tq), F32)
        for r in range(0, sk, CNT_ROWS):
            acc = jnp.where(sc_cur[r:r + CNT_ROWS, :] >= cand, acc + 1.0, acc)
        return jnp.sum(acc, axis=0, keepdims=True)

    def count_ge16(cand):
        cand = jnp.broadcast_to(cand.astype(BF16), (CNT_ROWS, tq))
        acc = jnp.zeros((CNT_ROWS, tq), BF16)
        for r in range(0, sk, CNT_ROWS):
            acc = jnp.where(s16_s[r:r + CNT_ROWS, :] >= cand, acc + 1.0, acc)
        return jnp.sum(acc.astype(F32), axis=0, keepdims=True)

    def mxu_slice(i):
        hd, part = i >> 1, i & 1
        r0 = pl.multiple_of(part * (sk // 2), LANES)
        lg_s[hd, pl.ds(r0, sk // 2), :] = _dot_nt(k_ref[hd & 1, pl.ds(r0, sk // 2), :],
                                                   q_ref[hd >> 1])
        if has_next:
            r0n = pl.multiple_of(part * (skn // 2), LANES)
            sc_nxt[pl.ds(r0n, skn // 2), :] += _indexer_rows(
                ki_ref, qin_ref, witn_ref, hd, r0n, skn // 2)

    def probe(count, t, bit):
        cand = t + jnp.left_shift(jnp.int32(1), bit)
        return jnp.where(count(_key_to_f32(cand)) >= kf, cand, t)

    def level1(i, t):
        return probe(count_ge16, probe(count_ge16, t, 31 - 2 * i), 30 - 2 * i)

    t16 = lax.fori_loop(0, N_HEADS, level1, jnp.full((1, tq), INT_MIN, jnp.int32))

    def level2(i, t):
        t = probe(count_ge, probe(count_ge, t, 16 - 2 * i), 15 - 2 * i)
        mxu_slice(2 * i)
        mxu_slice(2 * i + 1)
        return t

    base = jnp.maximum(t16, INT_MIN + BF16_STEP) - BF16_STEP
    tkey = probe(count_ge, lax.fori_loop(0, N_HEADS, level2, base), 0)
    tkey = jnp.maximum(tkey, KEY_NEG_INF)
    thr = _key_to_f32(tkey)

    n_ge = count_ge(thr)
    tie = jnp.where(tkey > KEY_NEG_INF, jnp.where(n_ge > kf, 1.0, 0.0), 0.0)
    any_tie = jnp.max(tie) > 0.0

    def tie_cut():
        sc = sc_cur[0:sk, :]
        need = kf - _colsum(jnp.where(sc > thr, 1.0, 0.0))
        eq = jnp.where(sc == thr, 1.0, 0.0)

        def step(i, c):
            cand = c + jnp.left_shift(jnp.int32(1), 11 - i)
            cnt = _colsum(jnp.where(kidx < cand, eq, 0.0))
            return jnp.where(cnt < need, cand, c)

        return lax.fori_loop(0, 12, step, jnp.zeros((1, tq), jnp.int32))

    cut = lax.cond(any_tie, tie_cut, lambda: jnp.full((1, tq), sk, jnp.int32))
    cut = jnp.minimum(cut, qidx)
    bounded = lmax_ref[0] <= MAX_SAFE_LOGIT
    shift = jnp.where(bounded, lmax_ref[0], 0.0)
    sc = sc_cur[0:sk, :]
    bias_s[0:sk, :] = jnp.where(
        sc > thr, -shift, jnp.where(sc == thr, jnp.where(kidx <= cut, -shift, NEG_BIG), NEG_BIG))

    @pl.when(jnp.logical_not(bounded))
    def _():
        def per_head(hd, carry):
            def blk(ci):
                return pl.ds(pl.multiple_of(ci * tq, tq), tq)

            def row_max(ci, m):
                s = lg_s[hd, blk(ci), :] + bias_s[blk(ci), :]
                return jnp.maximum(m, jnp.max(s, axis=0, keepdims=True))

            m = lax.fori_loop(0, j + 1, row_max, jnp.full((1, tq), -jnp.inf, F32))

            def sub(ci, c):
                lg_s[hd, blk(ci), :] = lg_s[hd, blk(ci), :] - m
                return c

            return lax.fori_loop(0, j + 1, sub, carry)

        lax.fori_loop(0, N_HEADS, per_head, 0)

    vt = vt_ref[:, 0:sk]
    outs = []
    for hd in range(N_HEADS):
        s = lg_s[hd, 0:sk, :] + bias_s[0:sk, :]
        o = _dot(vt, jnp.exp2(s).astype(BF16))
        outs.append(o[0:HEAD_DIM, :] / o[HEAD_DIM:HEAD_DIM + 1, :])
    y_ref[...] = jnp.concatenate(outs, axis=0).T.astype(BF16)


def _attn_kernel(*refs, nq, topk, idx_scale):
    j = pl.program_id(1)
    for c in range(nq):
        pl.when(j == c)(functools.partial(
            _attn_step, *refs, j=c, nq=nq, topk=topk, idx_scale=idx_scale))


def _attn(logit_bound, q, qi, wit, k, vt, ki, *, batch, seq, topk):
    tq = TQ_ATTN
    assert IDX_HEADS == N_HEADS and seq % tq == 0 and tq % LANES == 0
    nq = seq // tq
    n_pairs = ATTN_WIDTH // LANES
    blk = lambda b, j: b * nq + j
    nblk = lambda b, j: b * nq + jnp.minimum(j + 1, nq - 1)
    qspec = lambda f: pl.BlockSpec((n_pairs, tq, LANES), lambda b, j: (0, f(b, j), 0))
    wspec = lambda f: pl.BlockSpec((IDX_HEADS, tq), lambda b, j: (0, f(b, j)))
    kspec = pl.BlockSpec((2, seq, LANES), lambda b, j: (0, b, 0))
    return pl.pallas_call(
        functools.partial(_attn_kernel, nq=nq, topk=topk,
                          idx_scale=(IDX_DIM ** -0.5) * (IDX_HEADS ** -0.5)),
        grid=(batch, nq),
        in_specs=[
            pl.BlockSpec(memory_space=pltpu.SMEM),
            qspec(blk), qspec(blk), wspec(blk), qspec(nblk), wspec(nblk),
            kspec,
            pl.BlockSpec((LANES, seq), lambda b, j: (0, b)),
            kspec,
        ],
        out_specs=pl.BlockSpec((tq, ATTN_WIDTH), lambda b, j: (blk(b, j), 0)),
        out_shape=jax.ShapeDtypeStruct((batch * seq, ATTN_WIDTH), BF16),
        scratch_shapes=[pltpu.VMEM((seq, tq), F32), pltpu.VMEM((seq, tq), F32),
                        pltpu.VMEM((N_HEADS, seq, tq), F32),
                        pltpu.VMEM((seq, tq), F32), pltpu.VMEM((seq, tq), BF16)],
        compiler_params=pltpu.CompilerParams(
            dimension_semantics=("arbitrary", "arbitrary"), vmem_limit_bytes=VMEM_LIMIT),
        name="attn",
    )(logit_bound, q, qi, wit, qi, wit, k, vt, ki)


def _merge_rows(r0, rows, t, x_ref, ya_ref, g_ref, wug_ref, pw_ref, pb_ref, ps_ref, pa_ref, pp_ref,
                wo_ref, o_ref, ubuf):
    rs = slice(r0, r0 + rows)
    x = x_ref[rs, :]
    h = _rms(x, g_ref[...]).astype(BF16)
    z = _dot(h, wug_ref[...])
    u = z[:, 0:POOL_WIDTH]
    g_attn = z[:, POOL_WIDTH:POOL_WIDTH + D_MODEL]
    g_pool = z[:, POOL_WIDTH + D_MODEL:POOL_WIDTH + 2 * D_MODEL]

    base = MAX_WIN + r0
    ubuf[base:base + rows, :] = u
    tm = x_ref.shape[0]
    pos1 = (t * tm + r0 + 1 + lax.broadcasted_iota(jnp.int32, (rows, 1), 0)).astype(F32)
    pb = pb_ref[...]
    ps = ps_ref[...]
    y_pool = []
    for gi, win in enumerate(POOL_WINDOWS):
        sl = slice(gi * POOL_GROUP_DIM, (gi + 1) * POOL_GROUP_DIM)
        ug = u[:, sl]
        acc = ug
        for s in range(1, win):
            acc = acc + ubuf[base - s:base - s + rows, sl]
        pooled = (acc / jnp.minimum(pos1, float(win)) - ug).astype(BF16)
        mixed = _dot(pooled, pw_ref[gi]) + pb[:, sl]
        y_pool.append((mixed * ps[:, sl]).astype(BF16))
    y_pool = jnp.concatenate(y_pool, axis=1)

    merged = (jax.nn.sigmoid(g_attn) * _dot(ya_ref[rs, :], pa_ref[...])
              + jax.nn.sigmoid(g_pool) * _dot(y_pool, pp_ref[...]))
    o_ref[rs, :] = x + _dot(merged.astype(BF16), wo_ref[...])


def _merge_kernel(*refs):
    ubuf = refs[-1]
    tm = refs[0].shape[0]
    t = pl.program_id(1)

    @pl.when(t == 0)
    def _():
        ubuf[0:MAX_WIN, :] = jnp.zeros((MAX_WIN, POOL_WIDTH), F32)

    for r in range(0, tm, SUB_MERGE):
        _merge_rows(r, SUB_MERGE, t, *refs)
    ubuf[0:MAX_WIN, :] = ubuf[tm:tm + MAX_WIN, :]


def _merge(x2d, y_attn, g, w_ug, pool_w, pool_b, pool_scale, proj_attn, proj_pool, w_out,
           *, batch, seq):
    tm = TM_MERGE
    nt = seq // tm
    row = lambda b, i: (b * nt + i, 0)
    return pl.pallas_call(
        _merge_kernel,
        grid=(batch, nt),
        in_specs=[
            pl.BlockSpec((tm, D_MODEL), row),
            pl.BlockSpec((tm, ATTN_WIDTH), row),
            _const_spec((1, D_MODEL)),
            _const_spec((D_MODEL, POOL_WIDTH + 2 * D_MODEL)),
            _const_spec((POOL_GROUPS, POOL_GROUP_DIM, POOL_GROUP_DIM)),
            _const_spec((1, POOL_WIDTH)),
            _const_spec((1, POOL_WIDTH)),
            _const_spec((ATTN_WIDTH, D_MODEL)),
            _const_spec((POOL_WIDTH, D_MODEL)),
            _const_spec((D_MODEL, D_MODEL)),
        ],
        out_specs=pl.BlockSpec((tm, D_MODEL), row),
        out_shape=jax.ShapeDtypeStruct((batch * seq, D_MODEL), F32),
        scratch_shapes=[pltpu.VMEM((MAX_WIN + tm, POOL_WIDTH), F32)],
        compiler_params=pltpu.CompilerParams(
            dimension_semantics=("arbitrary", "arbitrary"), vmem_limit_bytes=VMEM_LIMIT),
        name="merge",
    )(x2d, y_attn, g, w_ug, pool_w, pool_b, pool_scale, proj_attn, proj_pool, w_out)


def _rope_tables(positions):
    inv_freq = ROPE_THETA ** (-jnp.arange(0, HEAD_DIM, 2, dtype=F32) / HEAD_DIM)
    per_row = LANES // QUARTER
    ang = (positions.astype(F32).reshape(-1, per_row, 1) * inv_freq).reshape(-1, LANES)
    cos, sin = lax.optimization_barrier((jnp.cos(ang), jnp.sin(ang)))
    return cos.reshape(-1, QUARTER), sin.reshape(-1, QUARTER)


def kernel(x, positions, ffn1_norm, ffn1_w1, ffn1_w3, ffn1_w2, mix_norm, w_in, q_norm, k_norm,
           pool_w, pool_b, pool_scale, proj_attn, proj_pool, w_out,
           ffn2_norm, ffn2_w1, ffn2_w3, ffn2_w2):
    batch, seq, _ = x.shape
    depth = ffn1_norm.shape[0]
    topk = min(TOPK_MAX, seq // 4)
    cos, sin = _rope_tables(positions)
    head_of_lane = _quarter_interleave(np.repeat(np.arange(N_HEADS), HEAD_DIM))
    head_mean = jnp.asarray(
        (head_of_lane[:, None] == head_of_lane[None, :]) * (1.0 / HEAD_DIM), BF16)
    bf = lambda w: w.astype(BF16)

    xt = x.reshape(batch * seq, D_MODEL)
    for l in range(depth):
        w_proj = _quarter_interleave(bf(w_in[l, :, 0:C_END]))
        w_ug = bf(w_in[l, :, C_U:])
        qg = _quarter_interleave(jnp.tile(q_norm[l], N_HEADS))[None]
        kg = _quarter_interleave(jnp.concatenate([k_norm[l], jnp.ones_like(k_norm[l])]))[None]
        xt = _ffn(xt, ffn1_norm[l][None], bf(ffn1_w1[l]), bf(ffn1_w3[l]), bf(ffn1_w2[l]))
        q, qi, k, vt, ki, wit = _proj(xt, mix_norm[l][None], w_proj, head_mean, qg, kg, cos, sin)
        logit_bound = (BOUND_MARGIN * HEAD_DIM * Q_SCALE * jnp.max(jnp.abs(q_norm[l]))
                       * jnp.max(jnp.abs(k_norm[l]))).reshape(1)
        y_attn = _attn(logit_bound, q, qi, wit, k, vt, ki, batch=batch, seq=seq, topk=topk)
        xt = _merge(xt, y_attn, mix_norm[l][None], w_ug, bf(pool_w[l]),
                    pool_b[l].reshape(1, POOL_WIDTH), pool_scale[l][None],
                    bf(proj_attn[l]), bf(proj_pool[l]), bf(w_out[l]), batch=batch, seq=seq)
        xt = _ffn(xt, ffn2_norm[l][None], bf(ffn2_w1[l]), bf(ffn2_w3[l]), bf(ffn2_w2[l]))
    return xt.reshape(batch, seq, D_MODEL)
```

```python
import functools
import math

import jax
import jax.numpy as jnp
import numpy as np
from jax import lax
from jax.experimental import pallas as pl
from jax.experimental.pallas import tpu as pltpu

D_MODEL = 1024
N_HEADS = 8
HEAD_DIM = 64
ATTN_WIDTH = N_HEADS * HEAD_DIM
IDX_HEADS = 8
IDX_DIM = 64
TOPK_MAX = 256
POOL_WINDOWS = (2, 4, 8, 16)
POOL_GROUPS = len(POOL_WINDOWS)
POOL_GROUP_DIM = 128
POOL_WIDTH = POOL_GROUPS * POOL_GROUP_DIM
D_FF = 2816
ROPE_THETA = 10000.0
EPS = 1e-6
SPLITS = (ATTN_WIDTH, HEAD_DIM, HEAD_DIM, IDX_HEADS * IDX_DIM, IDX_DIM, IDX_HEADS,
          POOL_WIDTH, D_MODEL, D_MODEL)

LANES = 128
SUBLANES = 8
MAX_WIN = max(POOL_WINDOWS)
INT_MIN = -(2 ** 31)
KEY_NEG_INF = INT_MIN + 0x7FFFFF
BF16_STEP = 1 << 16
NEG_BIG = -1e30
Q_SCALE = (HEAD_DIM ** -0.5) * math.log2(math.e)
BOUND_MARGIN = 1.02
MAX_SAFE_LOGIT = 60.0
VMEM_LIMIT = 56 * 1024 * 1024

TM_FFN = 512
SUB_FFN = 256
TM_PROJ = 512
SUB_PROJ = 128
TM_MERGE = 512
SUB_MERGE = 256
TQ_ATTN = 256
CNT_ROWS = 64

F32 = jnp.float32
BF16 = jnp.bfloat16

_PTS = [int(c) for c in np.cumsum((0,) + SPLITS)]
C_Q, C_KV, C_QI, C_KIW, C_U = _PTS[0], _PTS[1], _PTS[3], _PTS[4], _PTS[6]
C_WI_LANE = _PTS[5] - C_KIW
C_END = C_KIW + LANES
assert (C_KV % LANES, C_QI % LANES, C_KIW % LANES) == (0, 0, 0)
assert C_QI - C_KV == 2 * HEAD_DIM == LANES and C_WI_LANE == IDX_DIM


def _const_spec(shape):
    return pl.BlockSpec(shape, lambda *_: (0,) * len(shape), pipeline_mode=pl.Buffered(1))


def _rms(x, g):
    ms = jnp.mean(x * x, axis=-1, keepdims=True)
    return (x * lax.rsqrt(ms + EPS)) * g


def _dot(a, b):
    return jnp.dot(a, b, preferred_element_type=F32)


def _ffn_kernel(x_ref, g_ref, w1_ref, w3_ref, w2_ref, o_ref):
    for r in range(0, TM_FFN, SUB_FFN):
        x = x_ref[r:r + SUB_FFN, :]
        h = _rms(x, g_ref[...]).astype(BF16)
        a = _dot(h, w1_ref[...])
        b = _dot(h, w3_ref[...])
        act = ((a * jax.nn.sigmoid(a)) * b).astype(BF16)
        o_ref[r:r + SUB_FFN, :] = x + 0.5 * _dot(act, w2_ref[...])


def _ffn(x2d, g, w1, w3, w2):
    t, tm = x2d.shape[0], TM_FFN
    return pl.pallas_call(
        _ffn_kernel,
        grid=(t // tm,),
        in_specs=[
            pl.BlockSpec((tm, D_MODEL), lambda i: (i, 0)),
            _const_spec((1, D_MODEL)),
            _const_spec((D_MODEL, D_FF)),
            _const_spec((D_MODEL, D_FF)),
            _const_spec((D_FF, D_MODEL)),
        ],
        out_specs=pl.BlockSpec((tm, D_MODEL), lambda i: (i, 0)),
        out_shape=jax.ShapeDtypeStruct((t, D_MODEL), F32),
        compiler_params=pltpu.CompilerParams(
            dimension_semantics=("arbitrary",), vmem_limit_bytes=VMEM_LIMIT),
        name="ffn",
    )(x2d, g, w1, w3, w2)


QUARTER = HEAD_DIM // 2


def _quarter_interleave(a, axis=-1):
    axis = axis % a.ndim
    shp = a.shape
    n = shp[axis] // LANES
    a = a.reshape(shp[:axis] + (n, 2, 2, QUARTER) + shp[axis + 1:])
    return jnp.swapaxes(a, axis + 1, axis + 2).reshape(shp)


def _proj_rows(r0, rows, x_ref, g_ref, w_ref, hm_ref, qg_ref, kg_ref, cos_ref, sin_ref,
               q_ref, qi_ref, k_ref, vt_ref, ki_ref, wit_ref):
    rs = slice(r0, r0 + rows)
    h = _rms(x_ref[rs, :], g_ref[...]).astype(BF16)
    z = _dot(h, w_ref[...])
    c32 = cos_ref[rs, :]
    s32 = sin_ref[rs, :]
    cos = jnp.concatenate([c32, c32, c32, c32], axis=1)
    sin = jnp.concatenate([-s32, -s32, s32, s32], axis=1)
    lane = lax.broadcasted_iota(jnp.int32, (rows, LANES), 1)
    is_a = (lane & QUARTER) == 0

    def rope(x):
        return x * cos + pltpu.roll(x, HEAD_DIM, axis=1) * sin

    q = z[:, C_Q:C_Q + ATTN_WIDTH]
    q_ms = _dot((q * q).astype(BF16), hm_ref[...])
    qn = (q * lax.rsqrt(q_ms + EPS)) * qg_ref[...]
    for s in range(ATTN_WIDTH // LANES):
        sl = slice(s * LANES, (s + 1) * LANES)
        q_ref[s, rs, :] = (rope(qn[:, sl]) * Q_SCALE).astype(BF16)
        qi_ref[s, rs, :] = rope(z[:, C_QI + s * LANES:C_QI + (s + 1) * LANES]).astype(BF16)

    kv = z[:, C_KV:C_KV + LANES]
    ms = jnp.sum(jnp.where(is_a, kv * kv, 0.0), axis=-1, keepdims=True) * (1.0 / HEAD_DIM)
    k_a = jnp.where(is_a, rope((kv * lax.rsqrt(ms + EPS)) * kg_ref[...]), 0.0)
    k_ref[0, rs, :] = k_a.astype(BF16)
    k_ref[1, rs, :] = pltpu.roll(k_a, QUARTER, axis=1).astype(BF16)

    v = jnp.where(lane < QUARTER, pltpu.roll(kv, LANES - QUARTER, axis=1),
                  pltpu.roll(kv, HEAD_DIM, axis=1))
    vt_ref[:, rs] = jnp.where(lane < HEAD_DIM, v, 1.0).T.astype(BF16)

    kiw = z[:, C_KIW:C_KIW + LANES]
    ki_a = jnp.where(is_a, rope(kiw), 0.0)
    ki_ref[0, rs, :] = ki_a.astype(BF16)
    ki_ref[1, rs, :] = pltpu.roll(ki_a, QUARTER, axis=1).astype(BF16)
    wit_ref[:, rs] = kiw.T[QUARTER:QUARTER + IDX_HEADS, :]


def _proj_kernel(*refs):
    for r in range(TM_PROJ // SUB_PROJ):
        _proj_rows(r * SUB_PROJ, SUB_PROJ, *refs)


def _proj(x2d, g, w_p, head_mean, qg, kg, cos, sin):
    t, tm = x2d.shape[0], TM_PROJ
    row = lambda i: (i, 0)
    colb = lambda i: (0, i)
    n_pairs = ATTN_WIDTH // LANES
    stack = lambda n: (jax.ShapeDtypeStruct((n, t, LANES), BF16),
                       pl.BlockSpec((n, tm, LANES), lambda i: (0, i, 0)))
    (q_shape, q_spec), (k_shape, k_spec) = stack(n_pairs), stack(2)
    return pl.pallas_call(
        _proj_kernel,
        grid=(t // tm,),
        in_specs=[
            pl.BlockSpec((tm, D_MODEL), row),
            _const_spec((1, D_MODEL)),
            _const_spec((D_MODEL, C_END)),
            _const_spec((ATTN_WIDTH, ATTN_WIDTH)),
            _const_spec((1, ATTN_WIDTH)),
            _const_spec((1, LANES)),
            pl.BlockSpec((tm, QUARTER), row),
            pl.BlockSpec((tm, QUARTER), row),
        ],
        out_specs=[q_spec, q_spec, k_spec, pl.BlockSpec((LANES, tm), colb), k_spec,
                   pl.BlockSpec((IDX_HEADS, tm), colb)],
        out_shape=[q_shape, q_shape, k_shape, jax.ShapeDtypeStruct((LANES, t), BF16), k_shape,
                   jax.ShapeDtypeStruct((IDX_HEADS, t), F32)],
        compiler_params=pltpu.CompilerParams(
            dimension_semantics=("arbitrary",), vmem_limit_bytes=VMEM_LIMIT),
        name="proj",
    )(x2d, g, w_p, head_mean, qg, kg, cos, sin)


def _key_to_f32(key):
    return pltpu.bitcast(key ^ ((key >> 31) & 0x7FFFFFFF), F32)


def _colsum(x):
    n, tq = x.shape
    part = jnp.sum(x.reshape(n // 64, 64, tq), axis=0)
    return jnp.sum(part, axis=0, keepdims=True)


def _dot_nt(a, b):
    return lax.dot_general(a, b, (((1,), (1,)), ((), ())), preferred_element_type=F32)


def _indexer_rows(ki_ref, qi_ref, wit_ref, hd, r0, rows):
    d = _dot_nt(ki_ref[hd & 1, pl.ds(r0, rows), :], qi_ref[hd >> 1])
    return wit_ref[pl.ds(hd, 1), :] * jnp.maximum(d, 0.0)


def _attn_step(lmax_ref, q_ref, qi_ref, wit_ref, qin_ref, witn_ref, k_ref, vt_ref, ki_ref,
               y_ref, sc_a, sc_b, lg_s, bias_s, s16_s, *, j, nq, topk, idx_scale):
    tq = q_ref.shape[1]
    sk = (j + 1) * tq
    sc_cur, sc_nxt = (sc_a, sc_b) if j % 2 == 0 else (sc_b, sc_a)
    has_next = j + 1 < nq
    skn = (j + 2) * tq
    qidx = j * tq + lax.broadcasted_iota(jnp.int32, (1, tq), 1)
    kidx = lax.broadcasted_iota(jnp.int32, (sk, tq), 0)

    if j == 0:
        raw = jnp.zeros((sk, tq), F32)
        for hd in range(IDX_HEADS):
            raw = raw + _indexer_rows(ki_ref, qi_ref, wit_ref, hd, 0, sk)
    else:
        raw = sc_cur[0:sk, :]
    score = jnp.where(kidx <= qidx, raw * idx_scale, -jnp.inf)
    sc_cur[0:sk, :] = score
    s16_s[0:sk, :] = score.astype(BF16)
    if has_next:
        sc_nxt[0:skn, :] = jnp.zeros((skn, tq), F32)

    kf = float(topk)

    half = tq // 2
    live = sk - half

    def count_rows(ref, cand, dtype):
        acc = jnp.zeros((CNT_ROWS, tq), dtype)
        for r in range(0, live, CNT_ROWS):
            acc = jnp.where(ref[r:r + CNT_ROWS, :] >= cand, acc + 1.0, acc)
        hi, cand_hi = acc[:, half:], cand[:, half:]
        for r in range(live, sk, CNT_ROWS):
            hi = jnp.where(ref[r:r + CNT_ROWS, half:] >= cand_hi, hi + 1.0, hi)
        acc = jnp.concatenate([acc[:, :half], hi], axis=1)
        return jnp.sum(acc.astype(F32), axis=0, keepdims=True)

    def count_ge(cand):
        return count_rows(sc_cur, cand, F32)

    def count_ge16(cand):
        cand = jnp.broadcast_to(cand.astype(BF16), (CNT_ROWS, tq))
        return count_rows(s16_s, cand, BF16)

    def mxu_slice(i):
        hd, part = i >> 1, i & 1
        r0 = pl.multiple_of(part * (sk // 2), LANES)
        lg_s[hd, pl.ds(r0, sk // 2), :] = _dot_nt(k_ref[hd & 1, pl.ds(r0, sk // 2), :],
                                                   q_ref[hd >> 1])
        if has_next:
            r0n = pl.multiple_of(part * (skn // 2), LANES)
            sc_nxt[pl.ds(r0n, skn // 2), :] += _indexer_rows(
                ki_ref, qin_ref, witn_ref, hd, r0n, skn // 2)

    def probe(count, t, bit):
        cand = t + jnp.left_shift(jnp.int32(1), bit)
        return jnp.where(count(_key_to_f32(cand)) >= kf, cand, t)

    def level1(i, t):
        return probe(count_ge16, probe(count_ge16, t, 31 - 2 * i), 30 - 2 * i)

    def level2(i, t):
        t = probe(count_ge, probe(count_ge, t, 16 - 2 * i), 15 - 2 * i)
        mxu_slice(2 * i)
        mxu_slice(2 * i + 1)
        return t

    if sk <= topk:
        for i in range(2 * N_HEADS):
            mxu_slice(i)
        tkey = jnp.full((1, tq), KEY_NEG_INF, jnp.int32)
    else:
        t16 = lax.fori_loop(0, N_HEADS, level1, jnp.full((1, tq), INT_MIN, jnp.int32))
        base = jnp.maximum(t16, INT_MIN + BF16_STEP) - BF16_STEP
        tkey = probe(count_ge, lax.fori_loop(0, N_HEADS, level2, base), 0)
        tkey = jnp.maximum(tkey, KEY_NEG_INF)
    thr = _key_to_f32(tkey)

    n_ge = count_ge(thr)
    tie = jnp.where(tkey > KEY_NEG_INF, jnp.where(n_ge > kf, 1.0, 0.0), 0.0)
    any_tie = jnp.max(tie) > 0.0

    def tie_cut():
        sc = sc_cur[0:sk, :]
        need = kf - _colsum(jnp.where(sc > thr, 1.0, 0.0))
        eq = jnp.where(sc == thr, 1.0, 0.0)

        def step(i, c):
            cand = c + jnp.left_shift(jnp.int32(1), 11 - i)
            cnt = _colsum(jnp.where(kidx < cand, eq, 0.0))
            return jnp.where(cnt < need, cand, c)

        return lax.fori_loop(0, 12, step, jnp.zeros((1, tq), jnp.int32))

    cut = lax.cond(any_tie, tie_cut, lambda: jnp.full((1, tq), sk, jnp.int32))
    cut = jnp.minimum(cut, qidx)
    bounded = lmax_ref[0] <= MAX_SAFE_LOGIT
    shift = jnp.where(bounded, lmax_ref[0], 0.0)
    sc = sc_cur[0:sk, :]
    bias_s[0:sk, :] = jnp.where(
        sc > thr, -shift, jnp.where(sc == thr, jnp.where(kidx <= cut, -shift, NEG_BIG), NEG_BIG))

    @pl.when(jnp.logical_not(bounded))
    def _():
        def per_head(hd, carry):
            def blk(ci):
                return pl.ds(pl.multiple_of(ci * tq, tq), tq)

            def row_max(ci, m):
                s = lg_s[hd, blk(ci), :] + bias_s[blk(ci), :]
                return jnp.maximum(m, jnp.max(s, axis=0, keepdims=True))

            m = lax.fori_loop(0, j + 1, row_max, jnp.full((1, tq), -jnp.inf, F32))

            def sub(ci, c):
                lg_s[hd, blk(ci), :] = lg_s[hd, blk(ci), :] - m
                return c

            return lax.fori_loop(0, j + 1, sub, carry)

        lax.fori_loop(0, N_HEADS, per_head, 0)

    vt = vt_ref[:, 0:sk]
    outs = []
    for hd in range(N_HEADS):
        s = lg_s[hd, 0:sk, :] + bias_s[0:sk, :]
        o = _dot(vt, jnp.exp2(s).astype(BF16))
        outs.append(o[0:HEAD_DIM, :] / o[HEAD_DIM:HEAD_DIM + 1, :])
    y_ref[...] = jnp.concatenate(outs, axis=0).T.astype(BF16)


def _attn_kernel(*refs, nq, topk, idx_scale):
    j = pl.program_id(1)
    for c in range(nq):
        pl.when(j == c)(functools.partial(
            _attn_step, *refs, j=c, nq=nq, topk=topk, idx_scale=idx_scale))


def _attn(logit_bound, q, qi, wit, k, vt, ki, *, batch, seq, topk):
    tq = TQ_ATTN
    assert IDX_HEADS == N_HEADS and seq % tq == 0 and tq % LANES == 0
    nq = seq // tq
    n_pairs = ATTN_WIDTH // LANES
    blk = lambda b, j: b * nq + j
    nblk = lambda b, j: b * nq + jnp.minimum(j + 1, nq - 1)
    qspec = lambda f: pl.BlockSpec((n_pairs, tq, LANES), lambda b, j: (0, f(b, j), 0))
    wspec = lambda f: pl.BlockSpec((IDX_HEADS, tq), lambda b, j: (0, f(b, j)))
    kspec = pl.BlockSpec((2, seq, LANES), lambda b, j: (0, b, 0))
    return pl.pallas_call(
        functools.partial(_attn_kernel, nq=nq, topk=topk,
                          idx_scale=(IDX_DIM ** -0.5) * (IDX_HEADS ** -0.5)),
        grid=(batch, nq),
        in_specs=[
            pl.BlockSpec(memory_space=pltpu.SMEM),
            qspec(blk), qspec(blk), wspec(blk), qspec(nblk), wspec(nblk),
            kspec,
            pl.BlockSpec((LANES, seq), lambda b, j: (0, b)),
            kspec,
        ],
        out_specs=pl.BlockSpec((tq, ATTN_WIDTH), lambda b, j: (blk(b, j), 0)),
        out_shape=jax.ShapeDtypeStruct((batch * seq, ATTN_WIDTH), BF16),
        scratch_shapes=[pltpu.VMEM((seq, tq), F32), pltpu.VMEM((seq, tq), F32),
                        pltpu.VMEM((N_HEADS, seq, tq), F32),
                        pltpu.VMEM((seq, tq), F32), pltpu.VMEM((seq, tq), BF16)],
        compiler_params=pltpu.CompilerParams(
            dimension_semantics=("arbitrary", "arbitrary"), vmem_limit_bytes=VMEM_LIMIT),
        name="attn",
    )(logit_bound, q, qi, wit, qi, wit, k, vt, ki)


def _merge_rows(r0, rows, t, x_ref, ya_ref, g_ref, wug_ref, pw_ref, pb_ref, ps_ref, pa_ref, pp_ref,
                wo_ref, o_ref, ubuf):
    rs = slice(r0, r0 + rows)
    x = x_ref[rs, :]
    h = _rms(x, g_ref[...]).astype(BF16)
    z = _dot(h, wug_ref[...])
    u = z[:, 0:POOL_WIDTH]
    g_attn = z[:, POOL_WIDTH:POOL_WIDTH + D_MODEL]
    g_pool = z[:, POOL_WIDTH + D_MODEL:POOL_WIDTH + 2 * D_MODEL]

    base = MAX_WIN + r0
    ubuf[base:base + rows, :] = u
    tm = x_ref.shape[0]
    pos1 = (t * tm + r0 + 1 + lax.broadcasted_iota(jnp.int32, (rows, 1), 0)).astype(F32)
    pb = pb_ref[...]
    ps = ps_ref[...]
    y_pool = []
    for gi, win in enumerate(POOL_WINDOWS):
        sl = slice(gi * POOL_GROUP_DIM, (gi + 1) * POOL_GROUP_DIM)
        ug = u[:, sl]
        acc = ug
        for s in range(1, win):
            acc = acc + ubuf[base - s:base - s + rows, sl]
        pooled = (acc / jnp.minimum(pos1, float(win)) - ug).astype(BF16)
        mixed = _dot(pooled, pw_ref[gi]) + pb[:, sl]
        y_pool.append((mixed * ps[:, sl]).astype(BF16))
    y_pool = jnp.concatenate(y_pool, axis=1)

    merged = (jax.nn.sigmoid(g_attn) * _dot(ya_ref[rs, :], pa_ref[...])
              + jax.nn.sigmoid(g_pool) * _dot(y_pool, pp_ref[...]))
    o_ref[rs, :] = x + _dot(merged.astype(BF16), wo_ref[...])


def _merge_kernel(*refs):
    ubuf = refs[-1]
    tm = refs[0].shape[0]
    t = pl.program_id(1)

    @pl.when(t == 0)
    def _():
        ubuf[0:MAX_WIN, :] = jnp.zeros((MAX_WIN, POOL_WIDTH), F32)

    for r in range(0, tm, SUB_MERGE):
        _merge_rows(r, SUB_MERGE, t, *refs)
    ubuf[0:MAX_WIN, :] = ubuf[tm:tm + MAX_WIN, :]


def _merge(x2d, y_attn, g, w_ug, pool_w, pool_b, pool_scale, proj_attn, proj_pool, w_out,
           *, batch, seq):
    tm = TM_MERGE
    nt = seq // tm
    row = lambda b, i: (b * nt + i, 0)
    return pl.pallas_call(
        _merge_kernel,
        grid=(batch, nt),
        in_specs=[
            pl.BlockSpec((tm, D_MODEL), row),
            pl.BlockSpec((tm, ATTN_WIDTH), row),
            _const_spec((1, D_MODEL)),
            _const_spec((D_MODEL, POOL_WIDTH + 2 * D_MODEL)),
            _const_spec((POOL_GROUPS, POOL_GROUP_DIM, POOL_GROUP_DIM)),
            _const_spec((1, POOL_WIDTH)),
            _const_spec((1, POOL_WIDTH)),
            _const_spec((ATTN_WIDTH, D_MODEL)),
            _const_spec((POOL_WIDTH, D_MODEL)),
            _const_spec((D_MODEL, D_MODEL)),
        ],
        out_specs=pl.BlockSpec((tm, D_MODEL), row),
        out_shape=jax.ShapeDtypeStruct((batch * seq, D_MODEL), F32),
        scratch_shapes=[pltpu.VMEM((MAX_WIN + tm, POOL_WIDTH), F32)],
        compiler_params=pltpu.CompilerParams(
            dimension_semantics=("arbitrary", "arbitrary"), vmem_limit_bytes=VMEM_LIMIT),
        name="merge",
    )(x2d, y_attn, g, w_ug, pool_w, pool_b, pool_scale, proj_attn, proj_pool, w_out)


def _rope_tables(positions):
    inv_freq = ROPE_THETA ** (-jnp.arange(0, HEAD_DIM, 2, dtype=F32) / HEAD_DIM)
    per_row = LANES // QUARTER
    ang = (positions.astype(F32).reshape(-1, per_row, 1) * inv_freq).reshape(-1, LANES)
    cos, sin = lax.optimization_barrier((jnp.cos(ang), jnp.sin(ang)))
    return cos.reshape(-1, QUARTER), sin.reshape(-1, QUARTER)


def kernel(x, positions, ffn1_norm, ffn1_w1, ffn1_w3, ffn1_w2, mix_norm, w_in, q_norm, k_norm,
           pool_w, pool_b, pool_scale, proj_attn, proj_pool, w_out,
           ffn2_norm, ffn2_w1, ffn2_w3, ffn2_w2):
    batch, seq, _ = x.shape
    depth = ffn1_norm.shape[0]
    topk = min(TOPK_MAX, seq // 4)
    cos, sin = _rope_tables(positions)
    head_of_lane = _quarter_interleave(np.repeat(np.arange(N_HEADS), HEAD_DIM))
    head_mean = jnp.asarray(
        (head_of_lane[:, None] == head_of_lane[None, :]) * (1.0 / HEAD_DIM), BF16)
    bf = lambda w: w.astype(BF16)

    xt = x.reshape(batch * seq, D_MODEL)
    for l in range(depth):
        w_proj = _quarter_interleave(bf(w_in[l, :, 0:C_END]))
        w_ug = bf(w_in[l, :, C_U:])
        qg = _quarter_interleave(jnp.tile(q_norm[l], N_HEADS))[None]
        kg = _quarter_interleave(jnp.concatenate([k_norm[l], jnp.ones_like(k_norm[l])]))[None]
        xt = _ffn(xt, ffn1_norm[l][None], bf(ffn1_w1[l]), bf(ffn1_w3[l]), bf(ffn1_w2[l]))
        q, qi, k, vt, ki, wit = _proj(xt, mix_norm[l][None], w_proj, head_mean, qg, kg, cos, sin)
        logit_bound = (BOUND_MARGIN * HEAD_DIM * Q_SCALE * jnp.max(jnp.abs(q_norm[l]))
                       * jnp.max(jnp.abs(k_norm[l]))).reshape(1)
        y_attn = _attn(logit_bound, q, qi, wit, k, vt, ki, batch=batch, seq=seq, topk=topk)
        xt = _merge(xt, y_attn, mix_norm[l][None], w_ug, bf(pool_w[l]),
                    pool_b[l].reshape(1, POOL_WIDTH), pool_scale[l][None],
                    bf(proj_attn[l]), bf(proj_pool[l]), bf(w_out[l]), batch=batch, seq=seq)
        xt = _ffn(xt, ffn2_norm[l][None], bf(ffn2_w1[l]), bf(ffn2_w3[l]), bf(ffn2_w2[l]))
    return xt.reshape(batch, seq, D_MODEL)
```

```python
import functools
import math

import jax
import jax.numpy as jnp
import numpy as np
from jax import lax
from jax.experimental import pallas as pl
from jax.experimental.pallas import tpu as pltpu

D_MODEL = 1024
N_HEADS = 8
HEAD_DIM = 64
ATTN_WIDTH = N_HEADS * HEAD_DIM
IDX_HEADS = 8
IDX_DIM = 64
TOPK_MAX = 256
POOL_WINDOWS = (2, 4, 8, 16)
POOL_GROUPS = len(POOL_WINDOWS)
POOL_GROUP_DIM = 128
POOL_WIDTH = POOL_GROUPS * POOL_GROUP_DIM
D_FF = 2816
ROPE_THETA = 10000.0
EPS = 1e-6
SPLITS = (ATTN_WIDTH, HEAD_DIM, HEAD_DIM, IDX_HEADS * IDX_DIM, IDX_DIM, IDX_HEADS,
          POOL_WIDTH, D_MODEL, D_MODEL)

LANES = 128
SUBLANES = 8
MAX_WIN = max(POOL_WINDOWS)
INT_MIN = -(2 ** 31)
KEY_NEG_INF = INT_MIN + 0x7FFFFF
BF16_STEP = 1 << 16
NEG_BIG = -1e30
Q_SCALE = (HEAD_DIM ** -0.5) * math.log2(math.e)
BOUND_MARGIN = 1.02
MAX_SAFE_LOGIT = 60.0
VMEM_LIMIT = 56 * 1024 * 1024

TM_FFN = 1024
SUB_FFN = 256
TM_PROJ = 1024
SUB_PROJ = 256
TM_MERGE = 1024
SUB_MERGE = 256
TQ_ATTN = 256
CNT_ROWS = 64

F32 = jnp.float32
BF16 = jnp.bfloat16

_PTS = [int(c) for c in np.cumsum((0,) + SPLITS)]
C_Q, C_KV, C_QI, C_KIW, C_U = _PTS[0], _PTS[1], _PTS[3], _PTS[4], _PTS[6]
C_WI_LANE = _PTS[5] - C_KIW
C_END = C_KIW + LANES
assert (C_KV % LANES, C_QI % LANES, C_KIW % LANES) == (0, 0, 0)
assert C_QI - C_KV == 2 * HEAD_DIM == LANES and C_WI_LANE == IDX_DIM


def _const_spec(shape):
    return pl.BlockSpec(shape, lambda *_: (0,) * len(shape), pipeline_mode=pl.Buffered(1))


def _rms(x, g):
    ms = jnp.mean(x * x, axis=-1, keepdims=True)
    return (x * lax.rsqrt(ms + EPS)) * g


def _dot(a, b):
    return jnp.dot(a, b, preferred_element_type=F32)


def _ffn_kernel(x_ref, g_ref, w1_ref, w3_ref, w2_ref, o_ref):
    for r in range(0, TM_FFN, SUB_FFN):
        x = x_ref[r:r + SUB_FFN, :]
        h = _rms(x, g_ref[...]).astype(BF16)
        a = _dot(h, w1_ref[...])
        b = _dot(h, w3_ref[...])
        act = ((a * jax.nn.sigmoid(a)) * b).astype(BF16)
        o_ref[r:r + SUB_FFN, :] = x + 0.5 * _dot(act, w2_ref[...])


def _ffn(x2d, g, w1, w3, w2):
    t, tm = x2d.shape[0], TM_FFN
    return pl.pallas_call(
        _ffn_kernel,
        grid=(t // tm,),
        in_specs=[
            pl.BlockSpec((tm, D_MODEL), lambda i: (i, 0)),
            _const_spec((1, D_MODEL)),
            _const_spec((D_MODEL, D_FF)),
            _const_spec((D_MODEL, D_FF)),
            _const_spec((D_FF, D_MODEL)),
        ],
        out_specs=pl.BlockSpec((tm, D_MODEL), lambda i: (i, 0)),
        out_shape=jax.ShapeDtypeStruct((t, D_MODEL), F32),
        compiler_params=pltpu.CompilerParams(
            dimension_semantics=("arbitrary",), vmem_limit_bytes=VMEM_LIMIT),
        name="ffn",
    )(x2d, g, w1, w3, w2)


QUARTER = HEAD_DIM // 2


def _quarter_interleave(a, axis=-1):
    axis = axis % a.ndim
    shp = a.shape
    n = shp[axis] // LANES
    a = a.reshape(shp[:axis] + (n, 2, 2, QUARTER) + shp[axis + 1:])
    return jnp.swapaxes(a, axis + 1, axis + 2).reshape(shp)


def _proj_rows(r0, rows, x_ref, g_ref, w_ref, hm_ref, qg_ref, kg_ref, cos_ref, sin_ref,
               q_ref, qi_ref, k_ref, vt_ref, ki_ref, wit_ref):
    rs = slice(r0, r0 + rows)
    h = _rms(x_ref[rs, :], g_ref[...]).astype(BF16)
    z = _dot(h, w_ref[...])
    c32 = cos_ref[rs, :]
    s32 = sin_ref[rs, :]
    cos = jnp.concatenate([c32, c32, c32, c32], axis=1)
    sin = jnp.concatenate([-s32, -s32, s32, s32], axis=1)
    lane = lax.broadcasted_iota(jnp.int32, (rows, LANES), 1)
    is_a = (lane & QUARTER) == 0

    def rope(x):
        return x * cos + pltpu.roll(x, HEAD_DIM, axis=1) * sin

    q = z[:, C_Q:C_Q + ATTN_WIDTH]
    q_ms = _dot((q * q).astype(BF16), hm_ref[...])
    qn = (q * lax.rsqrt(q_ms + EPS)) * qg_ref[...]
    for s in range(ATTN_WIDTH // LANES):
        sl = slice(s * LANES, (s + 1) * LANES)
        q_ref[s, rs, :] = (rope(qn[:, sl]) * Q_SCALE).astype(BF16)
        qi_ref[s, rs, :] = rope(z[:, C_QI + s * LANES:C_QI + (s + 1) * LANES]).astype(BF16)

    kv = z[:, C_KV:C_KV + LANES]
    ms = jnp.sum(jnp.where(is_a, kv * kv, 0.0), axis=-1, keepdims=True) * (1.0 / HEAD_DIM)
    k_a = jnp.where(is_a, rope((kv * lax.rsqrt(ms + EPS)) * kg_ref[...]), 0.0)
    k_ref[0, rs, :] = k_a.astype(BF16)
    k_ref[1, rs, :] = pltpu.roll(k_a, QUARTER, axis=1).astype(BF16)

    v = jnp.where(lane < QUARTER, pltpu.roll(kv, LANES - QUARTER, axis=1),
                  pltpu.roll(kv, HEAD_DIM, axis=1))
    vt_ref[:, rs] = jnp.where(lane < HEAD_DIM, v, 1.0).T.astype(BF16)

    kiw = z[:, C_KIW:C_KIW + LANES]
    ki_a = jnp.where(is_a, rope(kiw), 0.0)
    ki_ref[0, rs, :] = ki_a.astype(BF16)
    ki_ref[1, rs, :] = pltpu.roll(ki_a, QUARTER, axis=1).astype(BF16)
    wit_ref[:, rs] = kiw.T[QUARTER:QUARTER + IDX_HEADS, :]


def _proj_kernel(*refs):
    for r in range(TM_PROJ // SUB_PROJ):
        _proj_rows(r * SUB_PROJ, SUB_PROJ, *refs)


def _proj(x2d, g, w_p, head_mean, qg, kg, cos, sin):
    t, tm = x2d.shape[0], TM_PROJ
    row = lambda i: (i, 0)
    colb = lambda i: (0, i)
    n_pairs = ATTN_WIDTH // LANES
    stack = lambda n: (jax.ShapeDtypeStruct((n, t, LANES), BF16),
                       pl.BlockSpec((n, tm, LANES), lambda i: (0, i, 0)))
    (q_shape, q_spec), (k_shape, k_spec) = stack(n_pairs), stack(2)
    return pl.pallas_call(
        _proj_kernel,
        grid=(t // tm,),
        in_specs=[
            pl.BlockSpec((tm, D_MODEL), row),
            _const_spec((1, D_MODEL)),
            _const_spec((D_MODEL, C_END)),
            _const_spec((ATTN_WIDTH, ATTN_WIDTH)),
            _const_spec((1, ATTN_WIDTH)),
            _const_spec((1, LANES)),
            pl.BlockSpec((tm, QUARTER), row),
            pl.BlockSpec((tm, QUARTER), row),
        ],
        out_specs=[q_spec, q_spec, k_spec, pl.BlockSpec((LANES, tm), colb), k_spec,
                   pl.BlockSpec((IDX_HEADS, tm), colb)],
        out_shape=[q_shape, q_shape, k_shape, jax.ShapeDtypeStruct((LANES, t), BF16), k_shape,
                   jax.ShapeDtypeStruct((IDX_HEADS, t), F32)],
        compiler_params=pltpu.CompilerParams(
            dimension_semantics=("arbitrary",), vmem_limit_bytes=VMEM_LIMIT),
        name="proj",
    )(x2d, g, w_p, head_mean, qg, kg, cos, sin)


def _key_to_f32(key):
    return pltpu.bitcast(key ^ ((key >> 31) & 0x7FFFFFFF), F32)


def _colsum(x):
    n, tq = x.shape
    part = jnp.sum(x.reshape(n // 64, 64, tq), axis=0)
    return jnp.sum(part, axis=0, keepdims=True)


def _dot_nt(a, b):
    return lax.dot_general(a, b, (((1,), (1,)), ((), ())), preferred_element_type=F32)


def _indexer_rows(ki_ref, qi_ref, wit_ref, hd, r0, rows):
    d = _dot_nt(ki_ref[hd & 1, pl.ds(r0, rows), :], qi_ref[hd >> 1])
    return wit_ref[pl.ds(hd, 1), :] * jnp.maximum(d, 0.0)


def _attn_step(lmax_ref, q_ref, qi_ref, wit_ref, qin_ref, witn_ref, k_ref, vt_ref, ki_ref,
               y_ref, sc_a, sc_b, lg_s, bias_s, s16_s, *, j, nq, topk, idx_scale):
    tq = q_ref.shape[1]
    sk = (j + 1) * tq
    sc_cur, sc_nxt = (sc_a, sc_b) if j % 2 == 0 else (sc_b, sc_a)
    has_next = j + 1 < nq
    skn = (j + 2) * tq
    qidx = j * tq + lax.broadcasted_iota(jnp.int32, (1, tq), 1)
    kidx = lax.broadcasted_iota(jnp.int32, (sk, tq), 0)

    if j == 0:
        raw = jnp.zeros((sk, tq), F32)
        for hd in range(IDX_HEADS):
            raw = raw + _indexer_rows(ki_ref, qi_ref, wit_ref, hd, 0, sk)
    else:
        raw = sc_cur[0:sk, :]
    score = jnp.where(kidx <= qidx, raw * idx_scale, -jnp.inf)
    sc_cur[0:sk, :] = score
    s16_s[0:sk, :] = score.astype(BF16)
    if has_next:
        sc_nxt[0:skn, :] = jnp.zeros((skn, tq), F32)

    kf = float(topk)

    half = tq // 2
    live = sk - half

    def count_rows(ref, cand, dtype):
        acc = jnp.zeros((CNT_ROWS, tq), dtype)
        for r in range(0, live, CNT_ROWS):
            acc = jnp.where(ref[r:r + CNT_ROWS, :] >= cand, acc + 1.0, acc)
        hi, cand_hi = acc[:, half:], cand[:, half:]
        for r in range(live, sk, CNT_ROWS):
            hi = jnp.where(ref[r:r + CNT_ROWS, half:] >= cand_hi, hi + 1.0, hi)
        acc = jnp.concatenate([acc[:, :half], hi], axis=1)
        return jnp.sum(acc.astype(F32), axis=0, keepdims=True)

    def count_ge(cand):
        return count_rows(sc_cur, cand, F32)

    def count_ge16(cand):
        cand = jnp.broadcast_to(cand.astype(BF16), (CNT_ROWS, tq))
        return count_rows(s16_s, cand, BF16)

    def mxu_slice(i):
        hd, part = i >> 1, i & 1
        r0 = pl.multiple_of(part * (sk // 2), LANES)
        lg_s[hd, pl.ds(r0, sk // 2), :] = _dot_nt(k_ref[hd & 1, pl.ds(r0, sk // 2), :],
                                                   q_ref[hd >> 1])
        if has_next:
            r0n = pl.multiple_of(part * (skn // 2), LANES)
            sc_nxt[pl.ds(r0n, skn // 2), :] += _indexer_rows(
                ki_ref, qin_ref, witn_ref, hd, r0n, skn // 2)

    def probe(count, t, bit):
        cand = t + jnp.left_shift(jnp.int32(1), bit)
        return jnp.where(count(_key_to_f32(cand)) >= kf, cand, t)

    def level1(i, t):
        return probe(count_ge16, probe(count_ge16, t, 31 - 2 * i), 30 - 2 * i)

    def level2(i, t):
        t = probe(count_ge, probe(count_ge, t, 16 - 2 * i), 15 - 2 * i)
        mxu_slice(2 * i)
        mxu_slice(2 * i + 1)
        return t

    if sk <= topk:
        for i in range(2 * N_HEADS):
            mxu_slice(i)
        tkey = jnp.full((1, tq), KEY_NEG_INF, jnp.int32)
    else:
        t16 = lax.fori_loop(0, N_HEADS, level1, jnp.full((1, tq), INT_MIN, jnp.int32))
        base = jnp.maximum(t16, INT_MIN + BF16_STEP) - BF16_STEP
        tkey = probe(count_ge, lax.fori_loop(0, N_HEADS, level2, base), 0)
        tkey = jnp.maximum(tkey, KEY_NEG_INF)
    thr = _key_to_f32(tkey)

    n_ge = count_ge(thr)
    tie = jnp.where(tkey > KEY_NEG_INF, jnp.where(n_ge > kf, 1.0, 0.0), 0.0)
    any_tie = jnp.max(tie) > 0.0

    def tie_cut():
        sc = sc_cur[0:sk, :]
        need = kf - _colsum(jnp.where(sc > thr, 1.0, 0.0))
        eq = jnp.where(sc == thr, 1.0, 0.0)

        def step(i, c):
            cand = c + jnp.left_shift(jnp.int32(1), 11 - i)
            cnt = _colsum(jnp.where(kidx < cand, eq, 0.0))
            return jnp.where(cnt < need, cand, c)

        return lax.fori_loop(0, 12, step, jnp.zeros((1, tq), jnp.int32))

    cut = lax.cond(any_tie, tie_cut, lambda: jnp.full((1, tq), sk, jnp.int32))
    cut = jnp.minimum(cut, qidx)
    bounded = lmax_ref[0] <= MAX_SAFE_LOGIT
    shift = jnp.where(bounded, lmax_ref[0], 0.0)
    sc = sc_cur[0:sk, :]
    bias_s[0:sk, :] = jnp.where(
        sc > thr, -shift, jnp.where(sc == thr, jnp.where(kidx <= cut, -shift, NEG_BIG), NEG_BIG))

    @pl.when(jnp.logical_not(bounded))
    def _():
        def per_head(hd, carry):
            def blk(ci):
                return pl.ds(pl.multiple_of(ci * tq, tq), tq)

            def row_max(ci, m):
                s = lg_s[hd, blk(ci), :] + bias_s[blk(ci), :]
                return jnp.maximum(m, jnp.max(s, axis=0, keepdims=True))

            m = lax.fori_loop(0, j + 1, row_max, jnp.full((1, tq), -jnp.inf, F32))

            def sub(ci, c):
                lg_s[hd, blk(ci), :] = lg_s[hd, blk(ci), :] - m
                return c

            return lax.fori_loop(0, j + 1, sub, carry)

        lax.fori_loop(0, N_HEADS, per_head, 0)

    vt = vt_ref[:, 0:sk]
    outs = []
    for hd in range(N_HEADS):
        s = lg_s[hd, 0:sk, :] + bias_s[0:sk, :]
        o = _dot(vt, jnp.exp2(s).astype(BF16))
        outs.append(o[0:HEAD_DIM, :] / o[HEAD_DIM:HEAD_DIM + 1, :])
    y_ref[...] = jnp.concatenate(outs, axis=0).T.astype(BF16)


def _attn_kernel(*refs, nq, topk, idx_scale):
    j = pl.program_id(1)
    for c in range(nq):
        pl.when(j == c)(functools.partial(
            _attn_step, *refs, j=c, nq=nq, topk=topk, idx_scale=idx_scale))


def _attn(logit_bound, q, qi, wit, k, vt, ki, *, batch, seq, topk):
    tq = TQ_ATTN
    assert IDX_HEADS == N_HEADS and seq % tq == 0 and tq % LANES == 0
    nq = seq // tq
    n_pairs = ATTN_WIDTH // LANES
    blk = lambda b, j: b * nq + j
    nblk = lambda b, j: b * nq + jnp.minimum(j + 1, nq - 1)
    qspec = lambda f: pl.BlockSpec((n_pairs, tq, LANES), lambda b, j: (0, f(b, j), 0))
    wspec = lambda f: pl.BlockSpec((IDX_HEADS, tq), lambda b, j: (0, f(b, j)))
    kspec = pl.BlockSpec((2, seq, LANES), lambda b, j: (0, b, 0))
    return pl.pallas_call(
        functools.partial(_attn_kernel, nq=nq, topk=topk,
                          idx_scale=(IDX_DIM ** -0.5) * (IDX_HEADS ** -0.5)),
        grid=(batch, nq),
        in_specs=[
            pl.BlockSpec(memory_space=pltpu.SMEM),
            qspec(blk), qspec(blk), wspec(blk), qspec(nblk), wspec(nblk),
            kspec,
            pl.BlockSpec((LANES, seq), lambda b, j: (0, b)),
            kspec,
        ],
        out_specs=pl.BlockSpec((tq, ATTN_WIDTH), lambda b, j: (blk(b, j), 0)),
        out_shape=jax.ShapeDtypeStruct((batch * seq, ATTN_WIDTH), BF16),
        scratch_shapes=[pltpu.VMEM((seq, tq), F32), pltpu.VMEM((seq, tq), F32),
                        pltpu.VMEM((N_HEADS, seq, tq), F32),
                        pltpu.VMEM((seq, tq), F32), pltpu.VMEM((seq, tq), BF16)],
        compiler_params=pltpu.CompilerParams(
            dimension_semantics=("arbitrary", "arbitrary"), vmem_limit_bytes=VMEM_LIMIT),
        name="attn",
    )(logit_bound, q, qi, wit, qi, wit, k, vt, ki)


def _merge_rows(r0, rows, t, x_ref, ya_ref, g_ref, wug_ref, pw_ref, pb_ref, ps_ref, pa_ref, pp_ref,
                wo_ref, o_ref, ubuf):
    rs = slice(r0, r0 + rows)
    x = x_ref[rs, :]
    h = _rms(x, g_ref[...]).astype(BF16)
    z = _dot(h, wug_ref[...])
    u = z[:, 0:POOL_WIDTH]
    g_attn = z[:, POOL_WIDTH:POOL_WIDTH + D_MODEL]
    g_pool = z[:, POOL_WIDTH + D_MODEL:POOL_WIDTH + 2 * D_MODEL]

    base = MAX_WIN + r0
    ubuf[base:base + rows, :] = u
    tm = x_ref.shape[0]
    pos1 = (t * tm + r0 + 1 + lax.broadcasted_iota(jnp.int32, (rows, 1), 0)).astype(F32)
    pb = pb_ref[...]
    ps = ps_ref[...]
    y_pool = []
    for gi, win in enumerate(POOL_WINDOWS):
        sl = slice(gi * POOL_GROUP_DIM, (gi + 1) * POOL_GROUP_DIM)
        ug = u[:, sl]
        acc = ug
        for s in range(1, win):
            acc = acc + ubuf[base - s:base - s + rows, sl]
        pooled = (acc / jnp.minimum(pos1, float(win)) - ug).astype(BF16)
        mixed = _dot(pooled, pw_ref[gi]) + pb[:, sl]
        y_pool.append((mixed * ps[:, sl]).astype(BF16))
    y_pool = jnp.concatenate(y_pool, axis=1)

    merged = (jax.nn.sigmoid(g_attn) * _dot(ya_ref[rs, :], pa_ref[...])
              + jax.nn.sigmoid(g_pool) * _dot(y_pool, pp_ref[...]))
    o_ref[rs, :] = x + _dot(merged.astype(BF16), wo_ref[...])


def _merge_kernel(*refs):
    ubuf = refs[-1]
    tm = refs[0].shape[0]
    t = pl.program_id(1)

    @pl.when(t == 0)
    def _():
        ubuf[0:MAX_WIN, :] = jnp.zeros((MAX_WIN, POOL_WIDTH), F32)

    for r in range(0, tm, SUB_MERGE):
        _merge_rows(r, SUB_MERGE, t, *refs)
    ubuf[0:MAX_WIN, :] = ubuf[tm:tm + MAX_WIN, :]


def _merge(x2d, y_attn, g, w_ug, pool_w, pool_b, pool_scale, proj_attn, proj_pool, w_out,
           *, batch, seq):
    tm = TM_MERGE
    nt = seq // tm
    row = lambda b, i: (b * nt + i, 0)
    return pl.pallas_call(
        _merge_kernel,
        grid=(batch, nt),
        in_specs=[
            pl.BlockSpec((tm, D_MODEL), row),
            pl.BlockSpec((tm, ATTN_WIDTH), row),
            _const_spec((1, D_MODEL)),
            _const_spec((D_MODEL, POOL_WIDTH + 2 * D_MODEL)),
            _const_spec((POOL_GROUPS, POOL_GROUP_DIM, POOL_GROUP_DIM)),
            _const_spec((1, POOL_WIDTH)),
            _const_spec((1, POOL_WIDTH)),
            _const_spec((ATTN_WIDTH, D_MODEL)),
            _const_spec((POOL_WIDTH, D_MODEL)),
            _const_spec((D_MODEL, D_MODEL)),
        ],
        out_specs=pl.BlockSpec((tm, D_MODEL), row),
        out_shape=jax.ShapeDtypeStruct((batch * seq, D_MODEL), F32),
        scratch_shapes=[pltpu.VMEM((MAX_WIN + tm, POOL_WIDTH), F32)],
        compiler_params=pltpu.CompilerParams(
            dimension_semantics=("arbitrary", "arbitrary"), vmem_limit_bytes=VMEM_LIMIT),
        name="merge",
    )(x2d, y_attn, g, w_ug, pool_w, pool_b, pool_scale, proj_attn, proj_pool, w_out)


def _rope_tables(positions):
    inv_freq = ROPE_THETA ** (-jnp.arange(0, HEAD_DIM, 2, dtype=F32) / HEAD_DIM)
    per_row = LANES // QUARTER
    ang = (positions.astype(F32).reshape(-1, per_row, 1) * inv_freq).reshape(-1, LANES)
    cos, sin = lax.optimization_barrier((jnp.cos(ang), jnp.sin(ang)))
    return cos.reshape(-1, QUARTER), sin.reshape(-1, QUARTER)


def kernel(x, positions, ffn1_norm, ffn1_w1, ffn1_w3, ffn1_w2, mix_norm, w_in, q_norm, k_norm,
           pool_w, pool_b, pool_scale, proj_attn, proj_pool, w_out,
           ffn2_norm, ffn2_w1, ffn2_w3, ffn2_w2):
    batch, seq, _ = x.shape
    depth = ffn1_norm.shape[0]
    topk = min(TOPK_MAX, seq // 4)
    cos, sin = _rope_tables(positions)
    head_of_lane = _quarter_interleave(np.repeat(np.arange(N_HEADS), HEAD_DIM))
    head_mean = jnp.asarray(
        (head_of_lane[:, None] == head_of_lane[None, :]) * (1.0 / HEAD_DIM), BF16)
    bf = lambda w: w.astype(BF16)

    xt = x.reshape(batch * seq, D_MODEL)
    for l in range(depth):
        w_proj = _quarter_interleave(bf(w_in[l, :, 0:C_END]))
        w_ug = bf(w_in[l, :, C_U:])
        qg = _quarter_interleave(jnp.tile(q_norm[l], N_HEADS))[None]
        kg = _quarter_interleave(jnp.concatenate([k_norm[l], jnp.ones_like(k_norm[l])]))[None]
        xt = _ffn(xt, ffn1_norm[l][None], bf(ffn1_w1[l]), bf(ffn1_w3[l]), bf(ffn1_w2[l]))
        q, qi, k, vt, ki, wit = _proj(xt, mix_norm[l][None], w_proj, head_mean, qg, kg, cos, sin)
        logit_bound = (BOUND_MARGIN * HEAD_DIM * Q_SCALE * jnp.max(jnp.abs(q_norm[l]))
                       * jnp.max(jnp.abs(k_norm[l]))).reshape(1)
        y_attn = _attn(logit_bound, q, qi, wit, k, vt, ki, batch=batch, seq=seq, topk=topk)
        xt = _merge(xt, y_attn, mix_norm[l][None], w_ug, bf(pool_w[l]),
                    pool_b[l].reshape(1, POOL_WIDTH), pool_scale[l][None],
                    bf(proj_attn[l]), bf(proj_pool[l]), bf(w_out[l]), batch=batch, seq=seq)
        xt = _ffn(xt, ffn2_norm[l][None], bf(ffn2_w1[l]), bf(ffn2_w3[l]), bf(ffn2_w2[l]))
    return xt.reshape(batch, seq, D_MODEL)
```

```python
import functools
import math

import jax
import jax.numpy as jnp
import numpy as np
from jax import lax
from jax.experimental import pallas as pl
from jax.experimental.pallas import tpu as pltpu

D_MODEL = 1024
N_HEADS = 8
HEAD_DIM = 64
ATTN_WIDTH = N_HEADS * HEAD_DIM
IDX_HEADS = 8
IDX_DIM = 64
TOPK_MAX = 256
POOL_WINDOWS = (2, 4, 8, 16)
POOL_GROUPS = len(POOL_WINDOWS)
POOL_GROUP_DIM = 128
POOL_WIDTH = POOL_GROUPS * POOL_GROUP_DIM
D_FF = 2816
ROPE_THETA = 10000.0
EPS = 1e-6
SPLITS = (ATTN_WIDTH, HEAD_DIM, HEAD_DIM, IDX_HEADS * IDX_DIM, IDX_DIM, IDX_HEADS,
          POOL_WIDTH, D_MODEL, D_MODEL)

LANES = 128
SUBLANES = 8
MAX_WIN = max(POOL_WINDOWS)
INT_MIN = -(2 ** 31)
KEY_NEG_INF = INT_MIN + 0x7FFFFF
BF16_STEP = 1 << 16
NEG_BIG = -1e30
Q_SCALE = (HEAD_DIM ** -0.5) * math.log2(math.e)
BOUND_MARGIN = 1.02
MAX_SAFE_LOGIT = 60.0
VMEM_LIMIT = 56 * 1024 * 1024

TM_FFN = 1024
SUB_FFN = 256
FFN_STAGE_IN_ROWS = 64
FFN_STAGE_OUT_ROWS = 176
assert D_MODEL % FFN_STAGE_IN_ROWS == 0 and D_FF % FFN_STAGE_OUT_ROWS == 0
TM_PROJ = 1024
SUB_PROJ = 256
TM_MERGE = 1024
SUB_MERGE = 256
TQ_ATTN = 256
CNT_ROWS = 64

F32 = jnp.float32
BF16 = jnp.bfloat16

_PTS = [int(c) for c in np.cumsum((0,) + SPLITS)]
C_Q, C_KV, C_QI, C_KIW, C_U = _PTS[0], _PTS[1], _PTS[3], _PTS[4], _PTS[6]
C_WI_LANE = _PTS[5] - C_KIW
C_END = C_KIW + LANES
assert (C_KV % LANES, C_QI % LANES, C_KIW % LANES) == (0, 0, 0)
assert C_QI - C_KV == 2 * HEAD_DIM == LANES and C_WI_LANE == IDX_DIM


def _const_spec(shape):
    return pl.BlockSpec(shape, lambda *_: (0,) * len(shape), pipeline_mode=pl.Buffered(1))


def _rms(x, g):
    ms = jnp.mean(x * x, axis=-1, keepdims=True)
    return (x * lax.rsqrt(ms + EPS)) * g


def _dot(a, b):
    return jnp.dot(a, b, preferred_element_type=F32)


def _stream_cast(src_hbm, dst_ref, stage, sem, rows):
    n = src_hbm.shape[0] // rows

    def copy(c):
        return pltpu.make_async_copy(src_hbm.at[pl.ds(c * rows, rows), :],
                                     stage.at[c % 2], sem.at[c % 2])

    copy(0).start()
    for c in range(n):
        if c + 1 < n:
            copy(c + 1).start()
        copy(c).wait()
        dst_ref[c * rows:(c + 1) * rows, :] = stage[c % 2].astype(BF16)


def _ffn_kernel(x_ref, g_ref, w1_hbm, w3_hbm, w2_hbm, o_ref,
                w1_ref, w3_ref, w2_ref, stage_in, stage_out, sem):
    @pl.when(pl.program_id(0) == 0)
    def _():
        _stream_cast(w1_hbm, w1_ref, stage_in, sem, FFN_STAGE_IN_ROWS)
        _stream_cast(w3_hbm, w3_ref, stage_in, sem, FFN_STAGE_IN_ROWS)
        _stream_cast(w2_hbm, w2_ref, stage_out, sem, FFN_STAGE_OUT_ROWS)

    for r in range(0, TM_FFN, SUB_FFN):
        x = x_ref[r:r + SUB_FFN, :]
        h = _rms(x, g_ref[...]).astype(BF16)
        a = _dot(h, w1_ref[...])
        b = _dot(h, w3_ref[...])
        act = ((a * jax.nn.sigmoid(a)) * b).astype(BF16)
        o_ref[r:r + SUB_FFN, :] = x + 0.5 * _dot(act, w2_ref[...])


def _ffn(x2d, g, w1, w3, w2):
    t, tm = x2d.shape[0], TM_FFN
    hbm = pl.BlockSpec(memory_space=pl.ANY)
    return pl.pallas_call(
        _ffn_kernel,
        grid=(t // tm,),
        in_specs=[
            pl.BlockSpec((tm, D_MODEL), lambda i: (i, 0)),
            _const_spec((1, D_MODEL)),
            hbm, hbm, hbm,
        ],
        out_specs=pl.BlockSpec((tm, D_MODEL), lambda i: (i, 0)),
        out_shape=jax.ShapeDtypeStruct((t, D_MODEL), F32),
        scratch_shapes=[
            pltpu.VMEM((D_MODEL, D_FF), BF16), pltpu.VMEM((D_MODEL, D_FF), BF16),
            pltpu.VMEM((D_FF, D_MODEL), BF16),
            pltpu.VMEM((2, FFN_STAGE_IN_ROWS, D_FF), F32),
            pltpu.VMEM((2, FFN_STAGE_OUT_ROWS, D_MODEL), F32),
            pltpu.SemaphoreType.DMA((2,)),
        ],
        compiler_params=pltpu.CompilerParams(
            dimension_semantics=("arbitrary",), vmem_limit_bytes=VMEM_LIMIT),
        name="ffn",
    )(x2d, g, w1, w3, w2)


QUARTER = HEAD_DIM // 2


def _quarter_interleave(a, axis=-1):
    axis = axis % a.ndim
    shp = a.shape
    n = shp[axis] // LANES
    a = a.reshape(shp[:axis] + (n, 2, 2, QUARTER) + shp[axis + 1:])
    return jnp.swapaxes(a, axis + 1, axis + 2).reshape(shp)


def _proj_rows(r0, rows, x_ref, g_ref, w_ref, hm_ref, qg_ref, kg_ref, cos_ref, sin_ref,
               q_ref, qi_ref, k_ref, vt_ref, ki_ref, wit_ref):
    rs = slice(r0, r0 + rows)
    h = _rms(x_ref[rs, :], g_ref[...]).astype(BF16)
    z = _dot(h, w_ref[...])
    c32 = cos_ref[rs, :]
    s32 = sin_ref[rs, :]
    cos = jnp.concatenate([c32, c32, c32, c32], axis=1)
    sin = jnp.concatenate([-s32, -s32, s32, s32], axis=1)
    lane = lax.broadcasted_iota(jnp.int32, (rows, LANES), 1)
    is_a = (lane & QUARTER) == 0

    def rope(x):
        return x * cos + pltpu.roll(x, HEAD_DIM, axis=1) * sin

    q = z[:, C_Q:C_Q + ATTN_WIDTH]
    q_ms = _dot((q * q).astype(BF16), hm_ref[...])
    qn = (q * lax.rsqrt(q_ms + EPS)) * qg_ref[...]
    for s in range(ATTN_WIDTH // LANES):
        sl = slice(s * LANES, (s + 1) * LANES)
        q_ref[s, rs, :] = (rope(qn[:, sl]) * Q_SCALE).astype(BF16)
        qi_ref[s, rs, :] = rope(z[:, C_QI + s * LANES:C_QI + (s + 1) * LANES]).astype(BF16)

    kv = z[:, C_KV:C_KV + LANES]
    ms = jnp.sum(jnp.where(is_a, kv * kv, 0.0), axis=-1, keepdims=True) * (1.0 / HEAD_DIM)
    k_a = jnp.where(is_a, rope((kv * lax.rsqrt(ms + EPS)) * kg_ref[...]), 0.0)
    k_ref[0, rs, :] = k_a.astype(BF16)
    k_ref[1, rs, :] = pltpu.roll(k_a, QUARTER, axis=1).astype(BF16)

    v = jnp.where(lane < QUARTER, pltpu.roll(kv, LANES - QUARTER, axis=1),
                  pltpu.roll(kv, HEAD_DIM, axis=1))
    vt_ref[:, rs] = jnp.where(lane < HEAD_DIM, v, 1.0).T.astype(BF16)

    kiw = z[:, C_KIW:C_KIW + LANES]
    ki_a = jnp.where(is_a, rope(kiw), 0.0)
    ki_ref[0, rs, :] = ki_a.astype(BF16)
    ki_ref[1, rs, :] = pltpu.roll(ki_a, QUARTER, axis=1).astype(BF16)
    wit_ref[:, rs] = kiw.T[QUARTER:QUARTER + IDX_HEADS, :]


def _proj_kernel(*refs):
    for r in range(TM_PROJ // SUB_PROJ):
        _proj_rows(r * SUB_PROJ, SUB_PROJ, *refs)


def _proj(x2d, g, w_p, head_mean, qg, kg, cos, sin):
    t, tm = x2d.shape[0], TM_PROJ
    row = lambda i: (i, 0)
    colb = lambda i: (0, i)
    n_pairs = ATTN_WIDTH // LANES
    stack = lambda n: (jax.ShapeDtypeStruct((n, t, LANES), BF16),
                       pl.BlockSpec((n, tm, LANES), lambda i: (0, i, 0)))
    (q_shape, q_spec), (k_shape, k_spec) = stack(n_pairs), stack(2)
    return pl.pallas_call(
        _proj_kernel,
        grid=(t // tm,),
        in_specs=[
            pl.BlockSpec((tm, D_MODEL), row),
            _const_spec((1, D_MODEL)),
            _const_spec((D_MODEL, C_END)),
            _const_spec((ATTN_WIDTH, ATTN_WIDTH)),
            _const_spec((1, ATTN_WIDTH)),
            _const_spec((1, LANES)),
            pl.BlockSpec((tm, QUARTER), row),
            pl.BlockSpec((tm, QUARTER), row),
        ],
        out_specs=[q_spec, q_spec, k_spec, pl.BlockSpec((LANES, tm), colb), k_spec,
                   pl.BlockSpec((IDX_HEADS, tm), colb)],
        out_shape=[q_shape, q_shape, k_shape, jax.ShapeDtypeStruct((LANES, t), BF16), k_shape,
                   jax.ShapeDtypeStruct((IDX_HEADS, t), F32)],
        compiler_params=pltpu.CompilerParams(
            dimension_semantics=("arbitrary",), vmem_limit_bytes=VMEM_LIMIT),
        name="proj",
    )(x2d, g, w_p, head_mean, qg, kg, cos, sin)


def _key_to_f32(key):
    return pltpu.bitcast(key ^ ((key >> 31) & 0x7FFFFFFF), F32)


def _colsum(x):
    n, tq = x.shape
    part = jnp.sum(x.reshape(n // 64, 64, tq), axis=0)
    return jnp.sum(part, axis=0, keepdims=True)


def _dot_nt(a, b):
    return lax.dot_general(a, b, (((1,), (1,)), ((), ())), preferred_element_type=F32)


def _indexer_rows(ki_ref, qi_ref, wit_ref, hd, r0, rows):
    d = _dot_nt(ki_ref[hd & 1, pl.ds(r0, rows), :], qi_ref[hd >> 1])
    return wit_ref[pl.ds(hd, 1), :] * jnp.maximum(d, 0.0)


def _attn_step(lmax_ref, q_ref, qi_ref, wit_ref, qin_ref, witn_ref, k_ref, vt_ref, ki_ref,
               y_ref, sc_a, sc_b, lg_s, bias_s, s16_s, *, j, nq, topk, idx_scale):
    tq = q_ref.shape[1]
    sk = (j + 1) * tq
    sc_cur, sc_nxt = (sc_a, sc_b) if j % 2 == 0 else (sc_b, sc_a)
    has_next = j + 1 < nq
    skn = (j + 2) * tq
    qidx = j * tq + lax.broadcasted_iota(jnp.int32, (1, tq), 1)
    kidx = lax.broadcasted_iota(jnp.int32, (sk, tq), 0)

    if j == 0:
        raw = jnp.zeros((sk, tq), F32)
        for hd in range(IDX_HEADS):
            raw = raw + _indexer_rows(ki_ref, qi_ref, wit_ref, hd, 0, sk)
    else:
        raw = sc_cur[0:sk, :]
    score = jnp.where(kidx <= qidx, raw * idx_scale, -jnp.inf)
    sc_cur[0:sk, :] = score
    s16_s[0:sk, :] = score.astype(BF16)
    if has_next:
        sc_nxt[0:skn, :] = jnp.zeros((skn, tq), F32)

    kf = float(topk)

    half = tq // 2
    live = sk - half

    def count_rows(ref, cand, dtype):
        acc = jnp.zeros((CNT_ROWS, tq), dtype)
        for r in range(0, live, CNT_ROWS):
            acc = jnp.where(ref[r:r + CNT_ROWS, :] >= cand, acc + 1.0, acc)
        hi, cand_hi = acc[:, half:], cand[:, half:]
        for r in range(live, sk, CNT_ROWS):
            hi = jnp.where(ref[r:r + CNT_ROWS, half:] >= cand_hi, hi + 1.0, hi)
        acc = jnp.concatenate([acc[:, :half], hi], axis=1)
        return jnp.sum(acc.astype(F32), axis=0, keepdims=True)

    def count_ge(cand):
        return count_rows(sc_cur, cand, F32)

    def count_ge16(cand):
        cand = jnp.broadcast_to(cand.astype(BF16), (CNT_ROWS, tq))
        return count_rows(s16_s, cand, BF16)

    def mxu_slice(i):
        hd, part = i >> 1, i & 1
        r0 = pl.multiple_of(part * (sk // 2), LANES)
        lg_s[hd, pl.ds(r0, sk // 2), :] = _dot_nt(k_ref[hd & 1, pl.ds(r0, sk // 2), :],
                                                   q_ref[hd >> 1])
        if has_next:
            r0n = pl.multiple_of(part * (skn // 2), LANES)
            sc_nxt[pl.ds(r0n, skn // 2), :] += _indexer_rows(
                ki_ref, qin_ref, witn_ref, hd, r0n, skn // 2)

    def probe(count, t, bit):
        cand = t + jnp.left_shift(jnp.int32(1), bit)
        return jnp.where(count(_key_to_f32(cand)) >= kf, cand, t)

    def level1(i, t):
        return probe(count_ge16, probe(count_ge16, t, 31 - 2 * i), 30 - 2 * i)

    def level2(i, t):
        t = probe(count_ge, probe(count_ge, t, 16 - 2 * i), 15 - 2 * i)
        mxu_slice(2 * i)
        mxu_slice(2 * i + 1)
        return t

    if sk <= topk:
        for i in range(2 * N_HEADS):
            mxu_slice(i)
        tkey = jnp.full((1, tq), KEY_NEG_INF, jnp.int32)
    else:
        t16 = lax.fori_loop(0, N_HEADS, level1, jnp.full((1, tq), INT_MIN, jnp.int32))
        base = jnp.maximum(t16, INT_MIN + BF16_STEP) - BF16_STEP
        tkey = probe(count_ge, lax.fori_loop(0, N_HEADS, level2, base), 0)
        tkey = jnp.maximum(tkey, KEY_NEG_INF)
    thr = _key_to_f32(tkey)

    n_ge = count_ge(thr)
    tie = jnp.where(tkey > KEY_NEG_INF, jnp.where(n_ge > kf, 1.0, 0.0), 0.0)
    any_tie = jnp.max(tie) > 0.0

    def tie_cut():
        sc = sc_cur[0:sk, :]
        need = kf - _colsum(jnp.where(sc > thr, 1.0, 0.0))
        eq = jnp.where(sc == thr, 1.0, 0.0)

        def step(i, c):
            cand = c + jnp.left_shift(jnp.int32(1), 11 - i)
            cnt = _colsum(jnp.where(kidx < cand, eq, 0.0))
            return jnp.where(cnt < need, cand, c)

        return lax.fori_loop(0, 12, step, jnp.zeros((1, tq), jnp.int32))

    cut = lax.cond(any_tie, tie_cut, lambda: jnp.full((1, tq), sk, jnp.int32))
    cut = jnp.minimum(cut, qidx)
    bounded = lmax_ref[0] <= MAX_SAFE_LOGIT
    shift = jnp.where(bounded, lmax_ref[0], 0.0)
    sc = sc_cur[0:sk, :]
    bias_s[0:sk, :] = jnp.where(
        sc > thr, -shift, jnp.where(sc == thr, jnp.where(kidx <= cut, -shift, NEG_BIG), NEG_BIG))

    @pl.when(jnp.logical_not(bounded))
    def _():
        def per_head(hd, carry):
            def blk(ci):
                return pl.ds(pl.multiple_of(ci * tq, tq), tq)

            def row_max(ci, m):
                s = lg_s[hd, blk(ci), :] + bias_s[blk(ci), :]
                return jnp.maximum(m, jnp.max(s, axis=0, keepdims=True))

            m = lax.fori_loop(0, j + 1, row_max, jnp.full((1, tq), -jnp.inf, F32))

            def sub(ci, c):
                lg_s[hd, blk(ci), :] = lg_s[hd, blk(ci), :] - m
                return c

            return lax.fori_loop(0, j + 1, sub, carry)

        lax.fori_loop(0, N_HEADS, per_head, 0)

    vt = vt_ref[:, 0:sk]
    outs = []
    for hd in range(N_HEADS):
        s = lg_s[hd, 0:sk, :] + bias_s[0:sk, :]
        o = _dot(vt, jnp.exp2(s).astype(BF16))
        outs.append(o[0:HEAD_DIM, :] / o[HEAD_DIM:HEAD_DIM + 1, :])
    y_ref[...] = jnp.concatenate(outs, axis=0).T.astype(BF16)


def _attn_kernel(*refs, nq, topk, idx_scale):
    j = pl.program_id(1)
    for c in range(nq):
        pl.when(j == c)(functools.partial(
            _attn_step, *refs, j=c, nq=nq, topk=topk, idx_scale=idx_scale))


def _attn(logit_bound, q, qi, wit, k, vt, ki, *, batch, seq, topk):
    tq = TQ_ATTN
    assert IDX_HEADS == N_HEADS and seq % tq == 0 and tq % LANES == 0
    nq = seq // tq
    n_pairs = ATTN_WIDTH // LANES
    blk = lambda b, j: b * nq + j
    nblk = lambda b, j: b * nq + jnp.minimum(j + 1, nq - 1)
    qspec = lambda f: pl.BlockSpec((n_pairs, tq, LANES), lambda b, j: (0, f(b, j), 0))
    wspec = lambda f: pl.BlockSpec((IDX_HEADS, tq), lambda b, j: (0, f(b, j)))
    kspec = pl.BlockSpec((2, seq, LANES), lambda b, j: (0, b, 0))
    return pl.pallas_call(
        functools.partial(_attn_kernel, nq=nq, topk=topk,
                          idx_scale=(IDX_DIM ** -0.5) * (IDX_HEADS ** -0.5)),
        grid=(batch, nq),
        in_specs=[
            pl.BlockSpec(memory_space=pltpu.SMEM),
            qspec(blk), qspec(blk), wspec(blk), qspec(nblk), wspec(nblk),
            kspec,
            pl.BlockSpec((LANES, seq), lambda b, j: (0, b)),
            kspec,
        ],
        out_specs=pl.BlockSpec((tq, ATTN_WIDTH), lambda b, j: (blk(b, j), 0)),
        out_shape=jax.ShapeDtypeStruct((batch * seq, ATTN_WIDTH), BF16),
        scratch_shapes=[pltpu.VMEM((seq, tq), F32), pltpu.VMEM((seq, tq), F32),
                        pltpu.VMEM((N_HEADS, seq, tq), F32),
                        pltpu.VMEM((seq, tq), F32), pltpu.VMEM((seq, tq), BF16)],
        compiler_params=pltpu.CompilerParams(
            dimension_semantics=("arbitrary", "arbitrary"), vmem_limit_bytes=VMEM_LIMIT),
        name="attn",
    )(logit_bound, q, qi, wit, qi, wit, k, vt, ki)


def _merge_rows(r0, rows, t, x_ref, ya_ref, g_ref, wug_ref, pw_ref, pb_ref, ps_ref, pa_ref, pp_ref,
                wo_ref, o_ref, ubuf):
    rs = slice(r0, r0 + rows)
    x = x_ref[rs, :]
    h = _rms(x, g_ref[...]).astype(BF16)
    z = _dot(h, wug_ref[...])
    u = z[:, 0:POOL_WIDTH]
    g_attn = z[:, POOL_WIDTH:POOL_WIDTH + D_MODEL]
    g_pool = z[:, POOL_WIDTH + D_MODEL:POOL_WIDTH + 2 * D_MODEL]

    base = MAX_WIN + r0
    ubuf[base:base + rows, :] = u
    tm = x_ref.shape[0]
    pos1 = (t * tm + r0 + 1 + lax.broadcasted_iota(jnp.int32, (rows, 1), 0)).astype(F32)
    pb = pb_ref[...]
    ps = ps_ref[...]
    y_pool = []
    for gi, win in enumerate(POOL_WINDOWS):
        sl = slice(gi * POOL_GROUP_DIM, (gi + 1) * POOL_GROUP_DIM)
        ug = u[:, sl]
        acc = ug
        for s in range(1, win):
            acc = acc + ubuf[base - s:base - s + rows, sl]
        pooled = (acc / jnp.minimum(pos1, float(win)) - ug).astype(BF16)
        mixed = _dot(pooled, pw_ref[gi]) + pb[:, sl]
        y_pool.append((mixed * ps[:, sl]).astype(BF16))
    y_pool = jnp.concatenate(y_pool, axis=1)

    merged = (jax.nn.sigmoid(g_attn) * _dot(ya_ref[rs, :], pa_ref[...])
              + jax.nn.sigmoid(g_pool) * _dot(y_pool, pp_ref[...]))
    o_ref[rs, :] = x + _dot(merged.astype(BF16), wo_ref[...])


def _merge_kernel(*refs):
    ubuf = refs[-1]
    tm = refs[0].shape[0]
    t = pl.program_id(1)

    @pl.when(t == 0)
    def _():
        ubuf[0:MAX_WIN, :] = jnp.zeros((MAX_WIN, POOL_WIDTH), F32)

    for r in range(0, tm, SUB_MERGE):
        _merge_rows(r, SUB_MERGE, t, *refs)
    ubuf[0:MAX_WIN, :] = ubuf[tm:tm + MAX_WIN, :]


def _merge(x2d, y_attn, g, w_ug, pool_w, pool_b, pool_scale, proj_attn, proj_pool, w_out,
           *, batch, seq):
    tm = TM_MERGE
    nt = seq // tm
    row = lambda b, i: (b * nt + i, 0)
    return pl.pallas_call(
        _merge_kernel,
        grid=(batch, nt),
        in_specs=[
            pl.BlockSpec((tm, D_MODEL), row),
            pl.BlockSpec((tm, ATTN_WIDTH), row),
            _const_spec((1, D_MODEL)),
            _const_spec((D_MODEL, POOL_WIDTH + 2 * D_MODEL)),
            _const_spec((POOL_GROUPS, POOL_GROUP_DIM, POOL_GROUP_DIM)),
            _const_spec((1, POOL_WIDTH)),
            _const_spec((1, POOL_WIDTH)),
            _const_spec((ATTN_WIDTH, D_MODEL)),
            _const_spec((POOL_WIDTH, D_MODEL)),
            _const_spec((D_MODEL, D_MODEL)),
        ],
        out_specs=pl.BlockSpec((tm, D_MODEL), row),
        out_shape=jax.ShapeDtypeStruct((batch * seq, D_MODEL), F32),
        scratch_shapes=[pltpu.VMEM((MAX_WIN + tm, POOL_WIDTH), F32)],
        compiler_params=pltpu.CompilerParams(
            dimension_semantics=("arbitrary", "arbitrary"), vmem_limit_bytes=VMEM_LIMIT),
        name="merge",
    )(x2d, y_attn, g, w_ug, pool_w, pool_b, pool_scale, proj_attn, proj_pool, w_out)


def _rope_tables(positions):
    inv_freq = ROPE_THETA ** (-jnp.arange(0, HEAD_DIM, 2, dtype=F32) / HEAD_DIM)
    per_row = LANES // QUARTER
    ang = (positions.astype(F32).reshape(-1, per_row, 1) * inv_freq).reshape(-1, LANES)
    cos, sin = lax.optimization_barrier((jnp.cos(ang), jnp.sin(ang)))
    return cos.reshape(-1, QUARTER), sin.reshape(-1, QUARTER)


def kernel(x, positions, ffn1_norm, ffn1_w1, ffn1_w3, ffn1_w2, mix_norm, w_in, q_norm, k_norm,
           pool_w, pool_b, pool_scale, proj_attn, proj_pool, w_out,
           ffn2_norm, ffn2_w1, ffn2_w3, ffn2_w2):
    batch, seq, _ = x.shape
    depth = ffn1_norm.shape[0]
    topk = min(TOPK_MAX, seq // 4)
    cos, sin = _rope_tables(positions)
    head_of_lane = _quarter_interleave(np.repeat(np.arange(N_HEADS), HEAD_DIM))
    head_mean = jnp.asarray(
        (head_of_lane[:, None] == head_of_lane[None, :]) * (1.0 / HEAD_DIM), BF16)
    bf = lambda w: w.astype(BF16)

    xt = x.reshape(batch * seq, D_MODEL)
    for l in range(depth):
        w_proj = _quarter_interleave(bf(w_in[l, :, 0:C_END]))
        w_ug = bf(w_in[l, :, C_U:])
        qg = _quarter_interleave(jnp.tile(q_norm[l], N_HEADS))[None]
        kg = _quarter_interleave(jnp.concatenate([k_norm[l], jnp.ones_like(k_norm[l])]))[None]
        xt = _ffn(xt, ffn1_norm[l][None], ffn1_w1[l], ffn1_w3[l], ffn1_w2[l])
        q, qi, k, vt, ki, wit = _proj(xt, mix_norm[l][None], w_proj, head_mean, qg, kg, cos, sin)
        logit_bound = (BOUND_MARGIN * HEAD_DIM * Q_SCALE * jnp.max(jnp.abs(q_norm[l]))
                       * jnp.max(jnp.abs(k_norm[l]))).reshape(1)
        y_attn = _attn(logit_bound, q, qi, wit, k, vt, ki, batch=batch, seq=seq, topk=topk)
        xt = _merge(xt, y_attn, mix_norm[l][None], w_ug, bf(pool_w[l]),
                    pool_b[l].reshape(1, POOL_WIDTH), pool_scale[l][None],
                    bf(proj_attn[l]), bf(proj_pool[l]), bf(w_out[l]), batch=batch, seq=seq)
        xt = _ffn(xt, ffn2_norm[l][None], ffn2_w1[l], ffn2_w3[l], ffn2_w2[l])
    return xt.reshape(batch, seq, D_MODEL)
```

```python
import functools
import math

import jax
import jax.numpy as jnp
import numpy as np
from jax import lax
from jax.experimental import pallas as pl
from jax.experimental.pallas import tpu as pltpu

D_MODEL = 1024
N_HEADS = 8
HEAD_DIM = 64
ATTN_WIDTH = N_HEADS * HEAD_DIM
IDX_HEADS = 8
IDX_DIM = 64
TOPK_MAX = 256
POOL_WINDOWS = (2, 4, 8, 16)
POOL_GROUPS = len(POOL_WINDOWS)
POOL_GROUP_DIM = 128
POOL_WIDTH = POOL_GROUPS * POOL_GROUP_DIM
D_FF = 2816
ROPE_THETA = 10000.0
EPS = 1e-6
SPLITS = (ATTN_WIDTH, HEAD_DIM, HEAD_DIM, IDX_HEADS * IDX_DIM, IDX_DIM, IDX_HEADS,
          POOL_WIDTH, D_MODEL, D_MODEL)

LANES = 128
SUBLANES = 8
MAX_WIN = max(POOL_WINDOWS)
INT_MIN = -(2 ** 31)
KEY_NEG_INF = INT_MIN + 0x7FFFFF
BF16_STEP = 1 << 16
NEG_BIG = -1e30
Q_SCALE = (HEAD_DIM ** -0.5) * math.log2(math.e)
BOUND_MARGIN = 1.02
MAX_SAFE_LOGIT = 60.0
VMEM_LIMIT = 56 * 1024 * 1024

TM_FFN = 1024
SUB_FFN = 256
TM_PROJ = 1024
SUB_PROJ = 256
TM_MERGE = 1024
SUB_MERGE = 256
TQ_ATTN = 256
CNT_ROWS = 64

F32 = jnp.float32
BF16 = jnp.bfloat16

_PTS = [int(c) for c in np.cumsum((0,) + SPLITS)]
C_Q, C_KV, C_QI, C_KIW, C_U = _PTS[0], _PTS[1], _PTS[3], _PTS[4], _PTS[6]
C_WI_LANE = _PTS[5] - C_KIW
C_END = C_KIW + LANES
assert (C_KV % LANES, C_QI % LANES, C_KIW % LANES) == (0, 0, 0)
assert C_QI - C_KV == 2 * HEAD_DIM == LANES and C_WI_LANE == IDX_DIM


def _const_spec(shape):
    return pl.BlockSpec(shape, lambda *_: (0,) * len(shape), pipeline_mode=pl.Buffered(1))


def _rms(x, g):
    ms = jnp.mean(x * x, axis=-1, keepdims=True)
    return (x * lax.rsqrt(ms + EPS)) * g


def _dot(a, b):
    return jnp.dot(a, b, preferred_element_type=F32)


def _ffn_kernel(x_ref, g_ref, w1_ref, w3_ref, w2_ref, o_ref):
    for r in range(0, TM_FFN, SUB_FFN):
        x = x_ref[r:r + SUB_FFN, :]
        h = _rms(x, g_ref[...]).astype(BF16)
        a = _dot(h, w1_ref[...])
        b = _dot(h, w3_ref[...])
        act = ((a * jax.nn.sigmoid(a)) * b).astype(BF16)
        o_ref[r:r + SUB_FFN, :] = x + 0.5 * _dot(act, w2_ref[...])


def _ffn(x2d, g, w1, w3, w2):
    t, tm = x2d.shape[0], TM_FFN
    return pl.pallas_call(
        _ffn_kernel,
        grid=(t // tm,),
        in_specs=[
            pl.BlockSpec((tm, D_MODEL), lambda i: (i, 0)),
            _const_spec((1, D_MODEL)),
            _const_spec((D_MODEL, D_FF)),
            _const_spec((D_MODEL, D_FF)),
            _const_spec((D_FF, D_MODEL)),
        ],
        out_specs=pl.BlockSpec((tm, D_MODEL), lambda i: (i, 0)),
        out_shape=jax.ShapeDtypeStruct((t, D_MODEL), F32),
        compiler_params=pltpu.CompilerParams(
            dimension_semantics=("arbitrary",), vmem_limit_bytes=VMEM_LIMIT),
        name="ffn",
    )(x2d, g, w1, w3, w2)


QUARTER = HEAD_DIM // 2


def _quarter_interleave(a, axis=-1):
    axis = axis % a.ndim
    shp = a.shape
    n = shp[axis] // LANES
    a = a.reshape(shp[:axis] + (n, 2, 2, QUARTER) + shp[axis + 1:])
    return jnp.swapaxes(a, axis + 1, axis + 2).reshape(shp)


def _proj_rows(r0, rows, x_ref, g_ref, w_ref, hm_ref, qg_ref, kg_ref, cos_ref, sin_ref,
               q_ref, qi_ref, k_ref, vt_ref, ki_ref, wit_ref):
    rs = slice(r0, r0 + rows)
    h = _rms(x_ref[rs, :], g_ref[...]).astype(BF16)
    z = _dot(h, w_ref[...])
    c32 = cos_ref[rs, :]
    s32 = sin_ref[rs, :]
    cos = jnp.concatenate([c32, c32, c32, c32], axis=1)
    sin = jnp.concatenate([-s32, -s32, s32, s32], axis=1)
    lane = lax.broadcasted_iota(jnp.int32, (rows, LANES), 1)
    is_a = (lane & QUARTER) == 0

    def rope(x):
        return x * cos + pltpu.roll(x, HEAD_DIM, axis=1) * sin

    q = z[:, C_Q:C_Q + ATTN_WIDTH]
    q_ms = _dot((q * q).astype(BF16), hm_ref[...])
    qn = (q * lax.rsqrt(q_ms + EPS)) * qg_ref[...]
    for s in range(ATTN_WIDTH // LANES):
        sl = slice(s * LANES, (s + 1) * LANES)
        q_ref[s, rs, :] = (rope(qn[:, sl]) * Q_SCALE).astype(BF16)
        qi_ref[s, rs, :] = rope(z[:, C_QI + s * LANES:C_QI + (s + 1) * LANES]).astype(BF16)

    kv = z[:, C_KV:C_KV + LANES]
    ms = jnp.sum(jnp.where(is_a, kv * kv, 0.0), axis=-1, keepdims=True) * (1.0 / HEAD_DIM)
    k_a = jnp.where(is_a, rope((kv * lax.rsqrt(ms + EPS)) * kg_ref[...]), 0.0)
    k_ref[0, rs, :] = k_a.astype(BF16)
    k_ref[1, rs, :] = pltpu.roll(k_a, QUARTER, axis=1).astype(BF16)

    v = jnp.where(lane < QUARTER, pltpu.roll(kv, LANES - QUARTER, axis=1),
                  pltpu.roll(kv, HEAD_DIM, axis=1))
    vt_ref[:, rs] = jnp.where(lane < HEAD_DIM, v, 1.0).T.astype(BF16)

    kiw = z[:, C_KIW:C_KIW + LANES]
    ki_a = jnp.where(is_a, rope(kiw), 0.0)
    ki_ref[0, rs, :] = ki_a.astype(BF16)
    ki_ref[1, rs, :] = pltpu.roll(ki_a, QUARTER, axis=1).astype(BF16)
    wit_ref[:, rs] = kiw.T[QUARTER:QUARTER + IDX_HEADS, :]


def _proj_kernel(*refs):
    for r in range(TM_PROJ // SUB_PROJ):
        _proj_rows(r * SUB_PROJ, SUB_PROJ, *refs)


def _proj(x2d, g, w_p, head_mean, qg, kg, cos, sin):
    t, tm = x2d.shape[0], TM_PROJ
    row = lambda i: (i, 0)
    colb = lambda i: (0, i)
    n_pairs = ATTN_WIDTH // LANES
    stack = lambda n: (jax.ShapeDtypeStruct((n, t, LANES), BF16),
                       pl.BlockSpec((n, tm, LANES), lambda i: (0, i, 0)))
    (q_shape, q_spec), (k_shape, k_spec) = stack(n_pairs), stack(2)
    return pl.pallas_call(
        _proj_kernel,
        grid=(t // tm,),
        in_specs=[
            pl.BlockSpec((tm, D_MODEL), row),
            _const_spec((1, D_MODEL)),
            _const_spec((D_MODEL, C_END)),
            _const_spec((ATTN_WIDTH, ATTN_WIDTH)),
            _const_spec((1, ATTN_WIDTH)),
            _const_spec((1, LANES)),
            pl.BlockSpec((tm, QUARTER), row),
            pl.BlockSpec((tm, QUARTER), row),
        ],
        out_specs=[q_spec, q_spec, k_spec, pl.BlockSpec((LANES, tm), colb), k_spec,
                   pl.BlockSpec((IDX_HEADS, tm), colb)],
        out_shape=[q_shape, q_shape, k_shape, jax.ShapeDtypeStruct((LANES, t), BF16), k_shape,
                   jax.ShapeDtypeStruct((IDX_HEADS, t), F32)],
        compiler_params=pltpu.CompilerParams(
            dimension_semantics=("arbitrary",), vmem_limit_bytes=VMEM_LIMIT),
        name="proj",
    )(x2d, g, w_p, head_mean, qg, kg, cos, sin)


def _key_to_f32(key):
    return pltpu.bitcast(key ^ ((key >> 31) & 0x7FFFFFFF), F32)


def _colsum(x):
    n, tq = x.shape
    part = jnp.sum(x.reshape(n // 64, 64, tq), axis=0)
    return jnp.sum(part, axis=0, keepdims=True)


def _dot_nt(a, b):
    return lax.dot_general(a, b, (((1,), (1,)), ((), ())), preferred_element_type=F32)


def _indexer_rows(ki_ref, qi_ref, wit_ref, hd, r0, rows):
    d = _dot_nt(ki_ref[hd & 1, pl.ds(r0, rows), :], qi_ref[hd >> 1])
    return wit_ref[pl.ds(hd, 1), :] * jnp.maximum(d, 0.0)


def _attn_step(lmax_ref, q_ref, qi_ref, wit_ref, qin_ref, witn_ref, k_ref, vt_ref, ki_ref,
               y_ref, sc_a, sc_b, lg_s, bias_s, s16_s, *, j, nq, topk, idx_scale):
    tq = q_ref.shape[1]
    sk = (j + 1) * tq
    sc_cur, sc_nxt = (sc_a, sc_b) if j % 2 == 0 else (sc_b, sc_a)
    has_next = j + 1 < nq
    skn = (j + 2) * tq
    qidx = j * tq + lax.broadcasted_iota(jnp.int32, (1, tq), 1)
    kidx = lax.broadcasted_iota(jnp.int32, (sk, tq), 0)

    if j == 0:
        raw = jnp.zeros((sk, tq), F32)
        for hd in range(IDX_HEADS):
            raw = raw + _indexer_rows(ki_ref, qi_ref, wit_ref, hd, 0, sk)
    else:
        raw = sc_cur[0:sk, :]
    score = jnp.where(kidx <= qidx, raw * idx_scale, -jnp.inf)
    sc_cur[0:sk, :] = score
    s16_s[0:sk, :] = score.astype(BF16)
    if has_next:
        sc_nxt[0:skn, :] = jnp.zeros((skn, tq), F32)

    kf = float(topk)

    half = tq // 2
    live = sk - half

    def count_rows(ref, cand, dtype):
        acc = jnp.zeros((CNT_ROWS, tq), dtype)
        for r in range(0, live, CNT_ROWS):
            acc = jnp.where(ref[r:r + CNT_ROWS, :] >= cand, acc + 1.0, acc)
        hi, cand_hi = acc[:, half:], cand[:, half:]
        for r in range(live, sk, CNT_ROWS):
            hi = jnp.where(ref[r:r + CNT_ROWS, half:] >= cand_hi, hi + 1.0, hi)
        acc = jnp.concatenate([acc[:, :half], hi], axis=1)
        return jnp.sum(acc.astype(F32), axis=0, keepdims=True)

    def count_ge(cand):
        return count_rows(sc_cur, cand, F32)

    def count_ge16(cand):
        cand = jnp.broadcast_to(cand.astype(BF16), (CNT_ROWS, tq))
        return count_rows(s16_s, cand, BF16)

    def mxu_slice(i):
        hd, part = i >> 1, i & 1
        r0 = pl.multiple_of(part * (sk // 2), LANES)
        lg_s[hd, pl.ds(r0, sk // 2), :] = _dot_nt(k_ref[hd & 1, pl.ds(r0, sk // 2), :],
                                                   q_ref[hd >> 1])
        if has_next:
            r0n = pl.multiple_of(part * (skn // 2), LANES)
            sc_nxt[pl.ds(r0n, skn // 2), :] += _indexer_rows(
                ki_ref, qin_ref, witn_ref, hd, r0n, skn // 2)

    def probe(count, t, bit):
        cand = t + jnp.left_shift(jnp.int32(1), bit)
        return jnp.where(count(_key_to_f32(cand)) >= kf, cand, t)

    def probe_counted(carry, bit):
        t, cnt = carry
        cand = t + jnp.left_shift(jnp.int32(1), bit)
        c = count_ge(_key_to_f32(cand))
        return jnp.where(c >= kf, cand, t), jnp.where(c >= kf, c, cnt)

    def level1(i, t):
        return probe(count_ge16, probe(count_ge16, t, 31 - 2 * i), 30 - 2 * i)

    def level2(i, carry):
        carry = probe_counted(probe_counted(carry, 16 - 2 * i), 15 - 2 * i)
        mxu_slice(2 * i)
        mxu_slice(2 * i + 1)
        return carry

    if sk <= topk:
        for i in range(2 * N_HEADS):
            mxu_slice(i)
        tkey = jnp.full((1, tq), KEY_NEG_INF, jnp.int32)
        n_ge = jnp.zeros((1, tq), F32)
    else:
        t16 = lax.fori_loop(0, N_HEADS, level1, jnp.full((1, tq), INT_MIN, jnp.int32))
        base = jnp.maximum(t16, INT_MIN + BF16_STEP) - BF16_STEP
        carry = lax.fori_loop(0, N_HEADS, level2, (base, jnp.full((1, tq), kf + 1.0, F32)))
        tkey, n_ge = probe_counted(carry, 0)
        tkey = jnp.maximum(tkey, KEY_NEG_INF)
    thr = _key_to_f32(tkey)

    tie = jnp.where(tkey > KEY_NEG_INF, jnp.where(n_ge > kf, 1.0, 0.0), 0.0)
    any_tie = jnp.max(tie) > 0.0

    def tie_cut():
        sc = sc_cur[0:sk, :]
        need = kf - _colsum(jnp.where(sc > thr, 1.0, 0.0))
        eq = jnp.where(sc == thr, 1.0, 0.0)

        def step(i, c):
            cand = c + jnp.left_shift(jnp.int32(1), 11 - i)
            cnt = _colsum(jnp.where(kidx < cand, eq, 0.0))
            return jnp.where(cnt < need, cand, c)

        return lax.fori_loop(0, 12, step, jnp.zeros((1, tq), jnp.int32))

    cut = lax.cond(any_tie, tie_cut, lambda: jnp.full((1, tq), sk, jnp.int32))
    cut = jnp.minimum(cut, qidx)
    bounded = lmax_ref[0] <= MAX_SAFE_LOGIT
    shift = jnp.where(bounded, lmax_ref[0], 0.0)
    sc = sc_cur[0:sk, :]
    bias_s[0:sk, :] = jnp.where(
        sc > thr, -shift, jnp.where(sc == thr, jnp.where(kidx <= cut, -shift, NEG_BIG), NEG_BIG))

    @pl.when(jnp.logical_not(bounded))
    def _():
        def per_head(hd, carry):
            def blk(ci):
                return pl.ds(pl.multiple_of(ci * tq, tq), tq)

            def row_max(ci, m):
                s = lg_s[hd, blk(ci), :] + bias_s[blk(ci), :]
                return jnp.maximum(m, jnp.max(s, axis=0, keepdims=True))

            m = lax.fori_loop(0, j + 1, row_max, jnp.full((1, tq), -jnp.inf, F32))

            def sub(ci, c):
                lg_s[hd, blk(ci), :] = lg_s[hd, blk(ci), :] - m
                return c

            return lax.fori_loop(0, j + 1, sub, carry)

        lax.fori_loop(0, N_HEADS, per_head, 0)

    vt = vt_ref[:, 0:sk]
    outs = []
    for hd in range(N_HEADS):
        s = lg_s[hd, 0:sk, :] + bias_s[0:sk, :]
        o = _dot(vt, jnp.exp2(s).astype(BF16))
        outs.append(o[0:HEAD_DIM, :] / o[HEAD_DIM:HEAD_DIM + 1, :])
    y_ref[...] = jnp.concatenate(outs, axis=0).T.astype(BF16)


def _attn_kernel(*refs, nq, topk, idx_scale):
    j = pl.program_id(1)
    for c in range(nq):
        pl.when(j == c)(functools.partial(
            _attn_step, *refs, j=c, nq=nq, topk=topk, idx_scale=idx_scale))


def _attn(logit_bound, q, qi, wit, k, vt, ki, *, batch, seq, topk):
    tq = TQ_ATTN
    assert IDX_HEADS == N_HEADS and seq % tq == 0 and tq % LANES == 0
    nq = seq // tq
    n_pairs = ATTN_WIDTH // LANES
    blk = lambda b, j: b * nq + j
    nblk = lambda b, j: b * nq + jnp.minimum(j + 1, nq - 1)
    qspec = lambda f: pl.BlockSpec((n_pairs, tq, LANES), lambda b, j: (0, f(b, j), 0))
    wspec = lambda f: pl.BlockSpec((IDX_HEADS, tq), lambda b, j: (0, f(b, j)))
    kspec = pl.BlockSpec((2, seq, LANES), lambda b, j: (0, b, 0))
    return pl.pallas_call(
        functools.partial(_attn_kernel, nq=nq, topk=topk,
                          idx_scale=(IDX_DIM ** -0.5) * (IDX_HEADS ** -0.5)),
        grid=(batch, nq),
        in_specs=[
            pl.BlockSpec(memory_space=pltpu.SMEM),
            qspec(blk), qspec(blk), wspec(blk), qspec(nblk), wspec(nblk),
            kspec,
            pl.BlockSpec((LANES, seq), lambda b, j: (0, b)),
            kspec,
        ],
        out_specs=pl.BlockSpec((tq, ATTN_WIDTH), lambda b, j: (blk(b, j), 0)),
        out_shape=jax.ShapeDtypeStruct((batch * seq, ATTN_WIDTH), BF16),
        scratch_shapes=[pltpu.VMEM((seq, tq), F32), pltpu.VMEM((seq, tq), F32),
                        pltpu.VMEM((N_HEADS, seq, tq), F32),
                        pltpu.VMEM((seq, tq), F32), pltpu.VMEM((seq, tq), BF16)],
        compiler_params=pltpu.CompilerParams(
            dimension_semantics=("arbitrary", "arbitrary"), vmem_limit_bytes=VMEM_LIMIT),
        name="attn",
    )(logit_bound, q, qi, wit, qi, wit, k, vt, ki)


def _merge_rows(r0, rows, t, x_ref, ya_ref, g_ref, wug_ref, pw_ref, pb_ref, ps_ref, pa_ref, pp_ref,
                wo_ref, o_ref, ubuf):
    rs = slice(r0, r0 + rows)
    x = x_ref[rs, :]
    h = _rms(x, g_ref[...]).astype(BF16)
    z = _dot(h, wug_ref[...])
    u = z[:, 0:POOL_WIDTH]
    g_attn = z[:, POOL_WIDTH:POOL_WIDTH + D_MODEL]
    g_pool = z[:, POOL_WIDTH + D_MODEL:POOL_WIDTH + 2 * D_MODEL]

    base = MAX_WIN + r0
    ubuf[base:base + rows, :] = u
    tm = x_ref.shape[0]
    pos1 = (t * tm + r0 + 1 + lax.broadcasted_iota(jnp.int32, (rows, 1), 0)).astype(F32)
    pb = pb_ref[...]
    ps = ps_ref[...]
    y_pool = []
    for gi, win in enumerate(POOL_WINDOWS):
        sl = slice(gi * POOL_GROUP_DIM, (gi + 1) * POOL_GROUP_DIM)
        ug = u[:, sl]
        acc = ug
        for s in range(1, win):
            acc = acc + ubuf[base - s:base - s + rows, sl]
        pooled = (acc / jnp.minimum(pos1, float(win)) - ug).astype(BF16)
        mixed = _dot(pooled, pw_ref[gi]) + pb[:, sl]
        y_pool.append((mixed * ps[:, sl]).astype(BF16))
    y_pool = jnp.concatenate(y_pool, axis=1)

    merged = (jax.nn.sigmoid(g_attn) * _dot(ya_ref[rs, :], pa_ref[...])
              + jax.nn.sigmoid(g_pool) * _dot(y_pool, pp_ref[...]))
    o_ref[rs, :] = x + _dot(merged.astype(BF16), wo_ref[...])


def _merge_kernel(*refs):
    ubuf = refs[-1]
    tm = refs[0].shape[0]
    t = pl.program_id(1)

    @pl.when(t == 0)
    def _():
        ubuf[0:MAX_WIN, :] = jnp.zeros((MAX_WIN, POOL_WIDTH), F32)

    for r in range(0, tm, SUB_MERGE):
        _merge_rows(r, SUB_MERGE, t, *refs)
    ubuf[0:MAX_WIN, :] = ubuf[tm:tm + MAX_WIN, :]


def _merge(x2d, y_attn, g, w_ug, pool_w, pool_b, pool_scale, proj_attn, proj_pool, w_out,
           *, batch, seq):
    tm = TM_MERGE
    nt = seq // tm
    row = lambda b, i: (b * nt + i, 0)
    return pl.pallas_call(
        _merge_kernel,
        grid=(batch, nt),
        in_specs=[
            pl.BlockSpec((tm, D_MODEL), row),
            pl.BlockSpec((tm, ATTN_WIDTH), row),
            _const_spec((1, D_MODEL)),
            _const_spec((D_MODEL, POOL_WIDTH + 2 * D_MODEL)),
            _const_spec((POOL_GROUPS, POOL_GROUP_DIM, POOL_GROUP_DIM)),
            _const_spec((1, POOL_WIDTH)),
            _const_spec((1, POOL_WIDTH)),
            _const_spec((ATTN_WIDTH, D_MODEL)),
            _const_spec((POOL_WIDTH, D_MODEL)),
            _const_spec((D_MODEL, D_MODEL)),
        ],
        out_specs=pl.BlockSpec((tm, D_MODEL), row),
        out_shape=jax.ShapeDtypeStruct((batch * seq, D_MODEL), F32),
        scratch_shapes=[pltpu.VMEM((MAX_WIN + tm, POOL_WIDTH), F32)],
        compiler_params=pltpu.CompilerParams(
            dimension_semantics=("arbitrary", "arbitrary"), vmem_limit_bytes=VMEM_LIMIT),
        name="merge",
    )(x2d, y_attn, g, w_ug, pool_w, pool_b, pool_scale, proj_attn, proj_pool, w_out)


def _rope_tables(positions):
    inv_freq = ROPE_THETA ** (-jnp.arange(0, HEAD_DIM, 2, dtype=F32) / HEAD_DIM)
    per_row = LANES // QUARTER
    ang = (positions.astype(F32).reshape(-1, per_row, 1) * inv_freq).reshape(-1, LANES)
    cos, sin = lax.optimization_barrier((jnp.cos(ang), jnp.sin(ang)))
    return cos.reshape(-1, QUARTER), sin.reshape(-1, QUARTER)


def kernel(x, positions, ffn1_norm, ffn1_w1, ffn1_w3, ffn1_w2, mix_norm, w_in, q_norm, k_norm,
           pool_w, pool_b, pool_scale, proj_attn, proj_pool, w_out,
           ffn2_norm, ffn2_w1, ffn2_w3, ffn2_w2):
    batch, seq, _ = x.shape
    depth = ffn1_norm.shape[0]
    topk = min(TOPK_MAX, seq // 4)
    cos, sin = _rope_tables(positions)
    head_of_lane = _quarter_interleave(np.repeat(np.arange(N_HEADS), HEAD_DIM))
    head_mean = jnp.asarray(
        (head_of_lane[:, None] == head_of_lane[None, :]) * (1.0 / HEAD_DIM), BF16)
    bf = lambda w: w.astype(BF16)

    xt = x.reshape(batch * seq, D_MODEL)
    for l in range(depth):
        w_proj = _quarter_interleave(bf(w_in[l, :, 0:C_END]))
        w_ug = bf(w_in[l, :, C_U:])
        qg = _quarter_interleave(jnp.tile(q_norm[l], N_HEADS))[None]
        kg = _quarter_interleave(jnp.concatenate([k_norm[l], jnp.ones_like(k_norm[l])]))[None]
        xt = _ffn(xt, ffn1_norm[l][None], bf(ffn1_w1[l]), bf(ffn1_w3[l]), bf(ffn1_w2[l]))
        q, qi, k, vt, ki, wit = _proj(xt, mix_norm[l][None], w_proj, head_mean, qg, kg, cos, sin)
        logit_bound = (BOUND_MARGIN * HEAD_DIM * Q_SCALE * jnp.max(jnp.abs(q_norm[l]))
                       * jnp.max(jnp.abs(k_norm[l]))).reshape(1)
        y_attn = _attn(logit_bound, q, qi, wit, k, vt, ki, batch=batch, seq=seq, topk=topk)
        xt = _merge(xt, y_attn, mix_norm[l][None], w_ug, bf(pool_w[l]),
                    pool_b[l].reshape(1, POOL_WIDTH), pool_scale[l][None],
                    bf(proj_attn[l]), bf(proj_pool[l]), bf(w_out[l]), batch=batch, seq=seq)
        xt = _ffn(xt, ffn2_norm[l][None], bf(ffn2_w1[l]), bf(ffn2_w3[l]), bf(ffn2_w2[l]))
    return xt.reshape(batch, seq, D_MODEL)
```

```python
import functools
import math

import jax
import jax.numpy as jnp
import numpy as np
from jax import lax
from jax.experimental import pallas as pl
from jax.experimental.pallas import tpu as pltpu

D_MODEL = 1024
N_HEADS = 8
HEAD_DIM = 64
ATTN_WIDTH = N_HEADS * HEAD_DIM
IDX_HEADS = 8
IDX_DIM = 64
TOPK_MAX = 256
POOL_WINDOWS = (2, 4, 8, 16)
POOL_GROUPS = len(POOL_WINDOWS)
POOL_GROUP_DIM = 128
POOL_WIDTH = POOL_GROUPS * POOL_GROUP_DIM
D_FF = 2816
ROPE_THETA = 10000.0
EPS = 1e-6
SPLITS = (ATTN_WIDTH, HEAD_DIM, HEAD_DIM, IDX_HEADS * IDX_DIM, IDX_DIM, IDX_HEADS,
          POOL_WIDTH, D_MODEL, D_MODEL)

LANES = 128
MAX_WIN = max(POOL_WINDOWS)
INT_MIN = -(2 ** 31)
KEY_NEG_INF = INT_MIN + 0x7FFFFF
BF16_STEP = 1 << 16
NEG_BIG = -1e30
Q_SCALE = (HEAD_DIM ** -0.5) * math.log2(math.e)
BOUND_MARGIN = 1.02
MAX_SAFE_LOGIT = 60.0
VMEM_LIMIT = 56 * 1024 * 1024

TM_FFN = 1024
SUB_FFN = 256
TM_PROJ = 1024
SUB_PROJ = 256
TM_MERGE = 1024
SUB_MERGE = 256
TQ_ATTN = 256
CNT_ROWS = 64

F32 = jnp.float32
BF16 = jnp.bfloat16

_PTS = [int(c) for c in np.cumsum((0,) + SPLITS)]
C_Q, C_KV, C_QI, C_KIW, C_U = _PTS[0], _PTS[1], _PTS[3], _PTS[4], _PTS[6]
C_WI_LANE = _PTS[5] - C_KIW
C_END = C_KIW + LANES
assert (C_KV % LANES, C_QI % LANES, C_KIW % LANES) == (0, 0, 0)
assert C_QI - C_KV == 2 * HEAD_DIM == LANES and C_WI_LANE == IDX_DIM


def _const_spec(shape):
    return pl.BlockSpec(shape, lambda *_: (0,) * len(shape), pipeline_mode=pl.Buffered(1))


def _rms(x, g):
    ms = jnp.mean(x * x, axis=-1, keepdims=True)
    return (x * lax.rsqrt(ms + EPS)) * g


def _dot(a, b):
    return jnp.dot(a, b, preferred_element_type=F32)


def _ffn_kernel(x_ref, g_ref, w1_ref, w3_ref, w2_ref, o_ref):
    for r in range(0, TM_FFN, SUB_FFN):
        x = x_ref[r:r + SUB_FFN, :]
        h = _rms(x, g_ref[...]).astype(BF16)
        a = _dot(h, w1_ref[...])
        b = _dot(h, w3_ref[...])
        act = ((a * jax.nn.sigmoid(a)) * b).astype(BF16)
        o_ref[r:r + SUB_FFN, :] = x + 0.5 * _dot(act, w2_ref[...])


def _ffn(x2d, g, w1, w3, w2):
    t, tm = x2d.shape[0], TM_FFN
    return pl.pallas_call(
        _ffn_kernel,
        grid=(t // tm,),
        in_specs=[
            pl.BlockSpec((tm, D_MODEL), lambda i: (i, 0)),
            _const_spec((1, D_MODEL)),
            _const_spec((D_MODEL, D_FF)),
            _const_spec((D_MODEL, D_FF)),
            _const_spec((D_FF, D_MODEL)),
        ],
        out_specs=pl.BlockSpec((tm, D_MODEL), lambda i: (i, 0)),
        out_shape=jax.ShapeDtypeStruct((t, D_MODEL), F32),
        compiler_params=pltpu.CompilerParams(
            dimension_semantics=("arbitrary",), vmem_limit_bytes=VMEM_LIMIT),
        name="ffn",
    )(x2d, g, w1, w3, w2)


QUARTER = HEAD_DIM // 2


def _quarter_interleave(a, axis=-1):
    axis = axis % a.ndim
    shp = a.shape
    n = shp[axis] // LANES
    a = a.reshape(shp[:axis] + (n, 2, 2, QUARTER) + shp[axis + 1:])
    return jnp.swapaxes(a, axis + 1, axis + 2).reshape(shp)


def _proj_rows(r0, rows, x_ref, g_ref, w_ref, hm_ref, qg_ref, kg_ref, cos_ref, sin_ref,
               q_ref, qi_ref, k_ref, vt_ref, ki_ref, wit_ref):
    rs = slice(r0, r0 + rows)
    h = _rms(x_ref[rs, :], g_ref[...]).astype(BF16)
    z = _dot(h, w_ref[...])
    c32 = cos_ref[rs, :]
    s32 = sin_ref[rs, :]
    cos = jnp.concatenate([c32, c32, c32, c32], axis=1)
    sin = jnp.concatenate([-s32, -s32, s32, s32], axis=1)
    lane = lax.broadcasted_iota(jnp.int32, (rows, LANES), 1)
    is_a = (lane & QUARTER) == 0

    def rope(x):
        return x * cos + pltpu.roll(x, HEAD_DIM, axis=1) * sin

    q = z[:, C_Q:C_Q + ATTN_WIDTH]
    q_ms = _dot((q * q).astype(BF16), hm_ref[...])
    qn = (q * lax.rsqrt(q_ms + EPS)) * qg_ref[...]
    for s in range(ATTN_WIDTH // LANES):
        sl = slice(s * LANES, (s + 1) * LANES)
        q_ref[s, rs, :] = (rope(qn[:, sl]) * Q_SCALE).astype(BF16)
        qi_ref[s, rs, :] = rope(z[:, C_QI + s * LANES:C_QI + (s + 1) * LANES]).astype(BF16)

    kv = z[:, C_KV:C_KV + LANES]
    ms = jnp.sum(jnp.where(is_a, kv * kv, 0.0), axis=-1, keepdims=True) * (1.0 / HEAD_DIM)
    k_a = jnp.where(is_a, rope((kv * lax.rsqrt(ms + EPS)) * kg_ref[...]), 0.0)
    k_ref[0, rs, :] = k_a.astype(BF16)
    k_ref[1, rs, :] = pltpu.roll(k_a, QUARTER, axis=1).astype(BF16)

    v = jnp.where(lane < QUARTER, pltpu.roll(kv, LANES - QUARTER, axis=1),
                  pltpu.roll(kv, HEAD_DIM, axis=1))
    vt_ref[:, rs] = jnp.where(lane < HEAD_DIM, v, 1.0).T.astype(BF16)

    kiw = z[:, C_KIW:C_KIW + LANES]
    ki_a = jnp.where(is_a, rope(kiw), 0.0)
    ki_ref[0, rs, :] = ki_a.astype(BF16)
    ki_ref[1, rs, :] = pltpu.roll(ki_a, QUARTER, axis=1).astype(BF16)
    wit_ref[:, rs] = kiw.T[QUARTER:QUARTER + IDX_HEADS, :]


def _proj_kernel(*refs):
    for r in range(TM_PROJ // SUB_PROJ):
        _proj_rows(r * SUB_PROJ, SUB_PROJ, *refs)


def _proj(x2d, g, w_p, head_mean, qg, kg, cos, sin):
    t, tm = x2d.shape[0], TM_PROJ
    row = lambda i: (i, 0)
    colb = lambda i: (0, i)
    n_pairs = ATTN_WIDTH // LANES
    stack = lambda n: (jax.ShapeDtypeStruct((n, t, LANES), BF16),
                       pl.BlockSpec((n, tm, LANES), lambda i: (0, i, 0)))
    (q_shape, q_spec), (k_shape, k_spec) = stack(n_pairs), stack(2)
    return pl.pallas_call(
        _proj_kernel,
        grid=(t // tm,),
        in_specs=[
            pl.BlockSpec((tm, D_MODEL), row),
            _const_spec((1, D_MODEL)),
            _const_spec((D_MODEL, C_END)),
            _const_spec((ATTN_WIDTH, ATTN_WIDTH)),
            _const_spec((1, ATTN_WIDTH)),
            _const_spec((1, LANES)),
            pl.BlockSpec((tm, QUARTER), row),
            pl.BlockSpec((tm, QUARTER), row),
        ],
        out_specs=[q_spec, q_spec, k_spec, pl.BlockSpec((LANES, tm), colb), k_spec,
                   pl.BlockSpec((IDX_HEADS, tm), colb)],
        out_shape=[q_shape, q_shape, k_shape, jax.ShapeDtypeStruct((LANES, t), BF16), k_shape,
                   jax.ShapeDtypeStruct((IDX_HEADS, t), F32)],
        compiler_params=pltpu.CompilerParams(
            dimension_semantics=("arbitrary",), vmem_limit_bytes=VMEM_LIMIT),
        name="proj",
    )(x2d, g, w_p, head_mean, qg, kg, cos, sin)


def _key_to_f32(key):
    return pltpu.bitcast(key ^ ((key >> 31) & 0x7FFFFFFF), F32)


def _colsum(x):
    n, tq = x.shape
    part = jnp.sum(x.reshape(n // CNT_ROWS, CNT_ROWS, tq), axis=0)
    return jnp.sum(part, axis=0, keepdims=True)


def _dot_nt(a, b):
    return lax.dot_general(a, b, (((1,), (1,)), ((), ())), preferred_element_type=F32)


def _indexer_rows(ki_ref, qi_ref, wit_ref, hd, r0, rows):
    d = _dot_nt(ki_ref[hd & 1, pl.ds(r0, rows), :], qi_ref[hd >> 1])
    return wit_ref[pl.ds(hd, 1), :] * jnp.maximum(d, 0.0)


def _attn_step(lmax_ref, q_ref, qi_ref, wit_ref, qin_ref, witn_ref, k_ref, vt_ref, ki_ref,
               y_ref, sc_a, sc_b, lg_s, bias_s, s16_s, *, j, nq, topk, idx_scale):
    tq = q_ref.shape[1]
    sk = (j + 1) * tq
    sc_cur, sc_nxt = (sc_a, sc_b) if j % 2 == 0 else (sc_b, sc_a)
    has_next = j + 1 < nq
    skn = (j + 2) * tq
    qidx = j * tq + lax.broadcasted_iota(jnp.int32, (1, tq), 1)
    kidx = lax.broadcasted_iota(jnp.int32, (sk, tq), 0)

    if j == 0:
        raw = jnp.zeros((sk, tq), F32)
        for hd in range(IDX_HEADS):
            raw = raw + _indexer_rows(ki_ref, qi_ref, wit_ref, hd, 0, sk)
    else:
        raw = sc_cur[0:sk, :]
    score = jnp.where(kidx <= qidx, raw * idx_scale, -jnp.inf)
    sc_cur[0:sk, :] = score
    s16_s[0:sk, :] = score.astype(BF16)
    if has_next:
        sc_nxt[0:skn, :] = jnp.zeros((skn, tq), F32)

    kf = float(topk)

    half = tq // 2
    live = sk - half

    def count_rows(ref, cand, dtype):
        acc = jnp.zeros((CNT_ROWS, tq), dtype)
        for r in range(0, live, CNT_ROWS):
            acc = jnp.where(ref[r:r + CNT_ROWS, :] >= cand, acc + 1.0, acc)
        hi, cand_hi = acc[:, half:], cand[:, half:]
        for r in range(live, sk, CNT_ROWS):
            hi = jnp.where(ref[r:r + CNT_ROWS, half:] >= cand_hi, hi + 1.0, hi)
        acc = jnp.concatenate([acc[:, :half], hi], axis=1)
        return jnp.sum(acc.astype(F32), axis=0, keepdims=True)

    def count_ge(cand):
        return count_rows(sc_cur, cand, F32)

    def count_ge16(cand):
        cand = jnp.broadcast_to(cand.astype(BF16), (CNT_ROWS, tq))
        return count_rows(s16_s, cand, BF16)

    def mxu_slice(i):
        hd, part = i >> 1, i & 1
        r0 = pl.multiple_of(part * (sk // 2), LANES)
        lg_s[hd, pl.ds(r0, sk // 2), :] = _dot_nt(k_ref[hd & 1, pl.ds(r0, sk // 2), :],
                                                   q_ref[hd >> 1])
        if has_next:
            r0n = pl.multiple_of(part * (skn // 2), LANES)
            sc_nxt[pl.ds(r0n, skn // 2), :] += _indexer_rows(
                ki_ref, qin_ref, witn_ref, hd, r0n, skn // 2)

    def probe(count, t, bit):
        cand = t + jnp.left_shift(jnp.int32(1), bit)
        return jnp.where(count(_key_to_f32(cand)) >= kf, cand, t)

    def probe_counted(carry, bit):
        t, cnt = carry
        cand = t + jnp.left_shift(jnp.int32(1), bit)
        c = count_ge(_key_to_f32(cand))
        return jnp.where(c >= kf, cand, t), jnp.where(c >= kf, c, cnt)

    def level1(i, t):
        return probe(count_ge16, probe(count_ge16, t, 31 - 2 * i), 30 - 2 * i)

    def level2(i, carry):
        carry = probe_counted(probe_counted(carry, 16 - 2 * i), 15 - 2 * i)
        mxu_slice(2 * i)
        mxu_slice(2 * i + 1)
        return carry

    if sk <= topk:
        for i in range(2 * N_HEADS):
            mxu_slice(i)
        tkey = jnp.full((1, tq), KEY_NEG_INF, jnp.int32)
        n_ge = jnp.zeros((1, tq), F32)
    else:
        t16 = lax.fori_loop(0, N_HEADS, level1, jnp.full((1, tq), INT_MIN, jnp.int32))
        base = jnp.maximum(t16, INT_MIN + BF16_STEP) - BF16_STEP
        carry = lax.fori_loop(0, N_HEADS, level2, (base, jnp.full((1, tq), kf + 1.0, F32)))
        tkey, n_ge = probe_counted(carry, 0)
        tkey = jnp.maximum(tkey, KEY_NEG_INF)
    thr = _key_to_f32(tkey)

    tie = jnp.where(tkey > KEY_NEG_INF, jnp.where(n_ge > kf, 1.0, 0.0), 0.0)
    any_tie = jnp.max(tie) > 0.0

    def tie_cut():
        sc = sc_cur[0:sk, :]
        need = kf - _colsum(jnp.where(sc > thr, 1.0, 0.0))
        eq = jnp.where(sc == thr, 1.0, 0.0)

        def step(i, c):
            cand = c + jnp.left_shift(jnp.int32(1), idx_bits - 1 - i)
            cnt = _colsum(jnp.where(kidx < cand, eq, 0.0))
            return jnp.where(cnt < need, cand, c)

        idx_bits = sk.bit_length()
        return lax.fori_loop(0, idx_bits, step, jnp.zeros((1, tq), jnp.int32))

    cut = lax.cond(any_tie, tie_cut, lambda: jnp.full((1, tq), sk, jnp.int32))
    cut = jnp.minimum(cut, qidx)
    bounded = lmax_ref[0] <= MAX_SAFE_LOGIT
    shift = jnp.where(bounded, lmax_ref[0], 0.0)
    sc = sc_cur[0:sk, :]
    bias_s[0:sk, :] = jnp.where(
        sc > thr, -shift, jnp.where(sc == thr, jnp.where(kidx <= cut, -shift, NEG_BIG), NEG_BIG))

    @pl.when(jnp.logical_not(bounded))
    def _():
        def per_head(hd, carry):
            def blk(ci):
                return pl.ds(pl.multiple_of(ci * tq, tq), tq)

            def row_max(ci, m):
                s = lg_s[hd, blk(ci), :] + bias_s[blk(ci), :]
                return jnp.maximum(m, jnp.max(s, axis=0, keepdims=True))

            m = lax.fori_loop(0, j + 1, row_max, jnp.full((1, tq), -jnp.inf, F32))

            def sub(ci, c):
                lg_s[hd, blk(ci), :] = lg_s[hd, blk(ci), :] - m
                return c

            return lax.fori_loop(0, j + 1, sub, carry)

        lax.fori_loop(0, N_HEADS, per_head, 0)

    vt = vt_ref[:, 0:sk]
    outs = []
    for hd in range(N_HEADS):
        s = lg_s[hd, 0:sk, :] + bias_s[0:sk, :]
        o = _dot(vt, jnp.exp2(s).astype(BF16))
        outs.append(o[0:HEAD_DIM, :] / o[HEAD_DIM:HEAD_DIM + 1, :])
    y_ref[...] = jnp.concatenate(outs, axis=0).T.astype(BF16)


def _attn_kernel(*refs, nq, topk, idx_scale):
    j = pl.program_id(1)
    for c in range(nq):
        pl.when(j == c)(functools.partial(
            _attn_step, *refs, j=c, nq=nq, topk=topk, idx_scale=idx_scale))


def _attn(logit_bound, q, qi, wit, k, vt, ki, *, batch, seq, topk):
    tq = TQ_ATTN
    assert IDX_HEADS == N_HEADS and seq % tq == 0 and tq % LANES == 0
    nq = seq // tq
    n_pairs = ATTN_WIDTH // LANES
    blk = lambda b, j: b * nq + j
    nblk = lambda b, j: b * nq + jnp.minimum(j + 1, nq - 1)
    qspec = lambda f: pl.BlockSpec((n_pairs, tq, LANES), lambda b, j: (0, f(b, j), 0))
    wspec = lambda f: pl.BlockSpec((IDX_HEADS, tq), lambda b, j: (0, f(b, j)))
    kspec = pl.BlockSpec((2, seq, LANES), lambda b, j: (0, b, 0))
    return pl.pallas_call(
        functools.partial(_attn_kernel, nq=nq, topk=topk,
                          idx_scale=(IDX_DIM ** -0.5) * (IDX_HEADS ** -0.5)),
        grid=(batch, nq),
        in_specs=[
            pl.BlockSpec(memory_space=pltpu.SMEM),
            qspec(blk), qspec(blk), wspec(blk), qspec(nblk), wspec(nblk),
            kspec,
            pl.BlockSpec((LANES, seq), lambda b, j: (0, b)),
            kspec,
        ],
        out_specs=pl.BlockSpec((tq, ATTN_WIDTH), lambda b, j: (blk(b, j), 0)),
        out_shape=jax.ShapeDtypeStruct((batch * seq, ATTN_WIDTH), BF16),
        scratch_shapes=[pltpu.VMEM((seq, tq), F32), pltpu.VMEM((seq, tq), F32),
                        pltpu.VMEM((N_HEADS, seq, tq), F32),
                        pltpu.VMEM((seq, tq), F32), pltpu.VMEM((seq, tq), BF16)],
        compiler_params=pltpu.CompilerParams(
            dimension_semantics=("arbitrary", "arbitrary"), vmem_limit_bytes=VMEM_LIMIT),
        name="attn",
    )(logit_bound, q, qi, wit, qi, wit, k, vt, ki)


def _merge_rows(r0, rows, t, x_ref, ya_ref, g_ref, wug_ref, pw_ref, pb_ref, ps_ref, pa_ref, pp_ref,
                wo_ref, o_ref, ubuf):
    rs = slice(r0, r0 + rows)
    x = x_ref[rs, :]
    h = _rms(x, g_ref[...]).astype(BF16)
    z = _dot(h, wug_ref[...])
    u = z[:, 0:POOL_WIDTH]
    g_attn = z[:, POOL_WIDTH:POOL_WIDTH + D_MODEL]
    g_pool = z[:, POOL_WIDTH + D_MODEL:POOL_WIDTH + 2 * D_MODEL]

    base = MAX_WIN + r0
    ubuf[base:base + rows, :] = u
    tm = x_ref.shape[0]
    pos1 = (t * tm + r0 + 1 + lax.broadcasted_iota(jnp.int32, (rows, 1), 0)).astype(F32)
    pb = pb_ref[...]
    ps = ps_ref[...]
    y_pool = []
    for gi, win in enumerate(POOL_WINDOWS):
        sl = slice(gi * POOL_GROUP_DIM, (gi + 1) * POOL_GROUP_DIM)
        ug = u[:, sl]
        acc = ug
        for s in range(1, win):
            acc = acc + ubuf[base - s:base - s + rows, sl]
        pooled = (acc / jnp.minimum(pos1, float(win)) - ug).astype(BF16)
        mixed = _dot(pooled, pw_ref[gi]) + pb[:, sl]
        y_pool.append((mixed * ps[:, sl]).astype(BF16))
    y_pool = jnp.concatenate(y_pool, axis=1)

    merged = (jax.nn.sigmoid(g_attn) * _dot(ya_ref[rs, :], pa_ref[...])
              + jax.nn.sigmoid(g_pool) * _dot(y_pool, pp_ref[...]))
    o_ref[rs, :] = x + _dot(merged.astype(BF16), wo_ref[...])


def _merge_kernel(*refs):
    ubuf = refs[-1]
    tm = refs[0].shape[0]
    t = pl.program_id(1)

    @pl.when(t == 0)
    def _():
        ubuf[0:MAX_WIN, :] = jnp.zeros((MAX_WIN, POOL_WIDTH), F32)

    for r in range(0, tm, SUB_MERGE):
        _merge_rows(r, SUB_MERGE, t, *refs)
    ubuf[0:MAX_WIN, :] = ubuf[tm:tm + MAX_WIN, :]


def _merge(x2d, y_attn, g, w_ug, pool_w, pool_b, pool_scale, proj_attn, proj_pool, w_out,
           *, batch, seq):
    tm = TM_MERGE
    nt = seq // tm
    row = lambda b, i: (b * nt + i, 0)
    return pl.pallas_call(
        _merge_kernel,
        grid=(batch, nt),
        in_specs=[
            pl.BlockSpec((tm, D_MODEL), row),
            pl.BlockSpec((tm, ATTN_WIDTH), row),
            _const_spec((1, D_MODEL)),
            _const_spec((D_MODEL, POOL_WIDTH + 2 * D_MODEL)),
            _const_spec((POOL_GROUPS, POOL_GROUP_DIM, POOL_GROUP_DIM)),
            _const_spec((1, POOL_WIDTH)),
            _const_spec((1, POOL_WIDTH)),
            _const_spec((ATTN_WIDTH, D_MODEL)),
            _const_spec((POOL_WIDTH, D_MODEL)),
            _const_spec((D_MODEL, D_MODEL)),
        ],
        out_specs=pl.BlockSpec((tm, D_MODEL), row),
        out_shape=jax.ShapeDtypeStruct((batch * seq, D_MODEL), F32),
        scratch_shapes=[pltpu.VMEM((MAX_WIN + tm, POOL_WIDTH), F32)],
        compiler_params=pltpu.CompilerParams(
            dimension_semantics=("arbitrary", "arbitrary"), vmem_limit_bytes=VMEM_LIMIT),
        name="merge",
    )(x2d, y_attn, g, w_ug, pool_w, pool_b, pool_scale, proj_attn, proj_pool, w_out)


def _rope_tables(positions):
    inv_freq = ROPE_THETA ** (-jnp.arange(0, HEAD_DIM, 2, dtype=F32) / HEAD_DIM)
    per_row = LANES // QUARTER
    ang = (positions.astype(F32).reshape(-1, per_row, 1) * inv_freq).reshape(-1, LANES)
    cos, sin = lax.optimization_barrier((jnp.cos(ang), jnp.sin(ang)))
    return cos.reshape(-1, QUARTER), sin.reshape(-1, QUARTER)


def kernel(x, positions, ffn1_norm, ffn1_w1, ffn1_w3, ffn1_w2, mix_norm, w_in, q_norm, k_norm,
           pool_w, pool_b, pool_scale, proj_attn, proj_pool, w_out,
           ffn2_norm, ffn2_w1, ffn2_w3, ffn2_w2):
    batch, seq, _ = x.shape
    depth = ffn1_norm.shape[0]
    topk = min(TOPK_MAX, seq // 4)
    cos, sin = _rope_tables(positions)
    head_of_lane = _quarter_interleave(np.repeat(np.arange(N_HEADS), HEAD_DIM))
    head_mean = jnp.asarray(
        (head_of_lane[:, None] == head_of_lane[None, :]) * (1.0 / HEAD_DIM), BF16)
    bf = lambda w: w.astype(BF16)

    xt = x.reshape(batch * seq, D_MODEL)
    for l in range(depth):
        w_proj = _quarter_interleave(bf(w_in[l, :, 0:C_END]))
        w_ug = bf(w_in[l, :, C_U:])
        qg = _quarter_interleave(jnp.tile(q_norm[l], N_HEADS))[None]
        kg = _quarter_interleave(jnp.concatenate([k_norm[l], jnp.ones_like(k_norm[l])]))[None]
        xt = _ffn(xt, ffn1_norm[l][None], bf(ffn1_w1[l]), bf(ffn1_w3[l]), bf(ffn1_w2[l]))
        q, qi, k, vt, ki, wit = _proj(xt, mix_norm[l][None], w_proj, head_mean, qg, kg, cos, sin)
        logit_bound = (BOUND_MARGIN * HEAD_DIM * Q_SCALE * jnp.max(jnp.abs(q_norm[l]))
                       * jnp.max(jnp.abs(k_norm[l]))).reshape(1)
        y_attn = _attn(logit_bound, q, qi, wit, k, vt, ki, batch=batch, seq=seq, topk=topk)
        xt = _merge(xt, y_attn, mix_norm[l][None], w_ug, bf(pool_w[l]),
                    pool_b[l].reshape(1, POOL_WIDTH), pool_scale[l][None],
                    bf(proj_attn[l]), bf(proj_pool[l]), bf(w_out[l]), batch=batch, seq=seq)
        xt = _ffn(xt, ffn2_norm[l][None], bf(ffn2_w1[l]), bf(ffn2_w3[l]), bf(ffn2_w2[l]))
    return xt.reshape(batch, seq, D_MODEL)
```

```python
import functools
import math

import jax
import jax.numpy as jnp
import numpy as np
from jax import lax
from jax.experimental import pallas as pl
from jax.experimental.pallas import tpu as pltpu

D_MODEL = 1024
N_HEADS = 8
HEAD_DIM = 64
ATTN_WIDTH = N_HEADS * HEAD_DIM
IDX_HEADS = 8
IDX_DIM = 64
TOPK_MAX = 256
POOL_WINDOWS = (2, 4, 8, 16)
POOL_GROUPS = len(POOL_WINDOWS)
POOL_GROUP_DIM = 128
POOL_WIDTH = POOL_GROUPS * POOL_GROUP_DIM
D_FF = 2816
ROPE_THETA = 10000.0
EPS = 1e-6
SPLITS = (ATTN_WIDTH, HEAD_DIM, HEAD_DIM, IDX_HEADS * IDX_DIM, IDX_DIM, IDX_HEADS,
          POOL_WIDTH, D_MODEL, D_MODEL)

LANES = 128
MAX_WIN = max(POOL_WINDOWS)
INT_MIN = -(2 ** 31)
KEY_NEG_INF = INT_MIN + 0x7FFFFF
BF16_STEP = 1 << 16
NEG_BIG = -1e30
Q_SCALE = (HEAD_DIM ** -0.5) * math.log2(math.e)
BOUND_MARGIN = 1.02
MAX_SAFE_LOGIT = 60.0
VMEM_LIMIT = 56 * 1024 * 1024

TM_FFN = 1024
SUB_FFN = 256
TM_PROJ = 1024
SUB_PROJ = 256
TM_MERGE = 1024
SUB_MERGE = 256
TQ_ATTN = 256
CNT_ROWS = 64

F32 = jnp.float32
BF16 = jnp.bfloat16

_PTS = [int(c) for c in np.cumsum((0,) + SPLITS)]
C_Q, C_KV, C_QI, C_KIW, C_U = _PTS[0], _PTS[1], _PTS[3], _PTS[4], _PTS[6]
C_WI_LANE = _PTS[5] - C_KIW
C_END = C_KIW + LANES
assert (C_KV % LANES, C_QI % LANES, C_KIW % LANES) == (0, 0, 0)
assert C_QI - C_KV == 2 * HEAD_DIM == LANES and C_WI_LANE == IDX_DIM


def _const_spec(shape):
    return pl.BlockSpec(shape, lambda *_: (0,) * len(shape), pipeline_mode=pl.Buffered(1))


def _rms(x, g):
    ms = jnp.mean(x * x, axis=-1, keepdims=True)
    return (x * lax.rsqrt(ms + EPS)) * g


def _dot(a, b):
    return jnp.dot(a, b, preferred_element_type=F32)


def _ffn_kernel(x_ref, g_ref, w1_ref, w3_ref, w2_ref, o_ref):
    for r in range(0, TM_FFN, SUB_FFN):
        x = x_ref[r:r + SUB_FFN, :]
        h = _rms(x, g_ref[...]).astype(BF16)
        a = _dot(h, w1_ref[...])
        b = _dot(h, w3_ref[...])
        act = ((a * jax.nn.sigmoid(a)) * b).astype(BF16)
        o_ref[r:r + SUB_FFN, :] = x + 0.5 * _dot(act, w2_ref[...])


def _ffn(x2d, g, w1, w3, w2):
    t, tm = x2d.shape[0], TM_FFN
    return pl.pallas_call(
        _ffn_kernel,
        grid=(t // tm,),
        in_specs=[
            pl.BlockSpec((tm, D_MODEL), lambda i: (i, 0)),
            _const_spec((1, D_MODEL)),
            _const_spec((D_MODEL, D_FF)),
            _const_spec((D_MODEL, D_FF)),
            _const_spec((D_FF, D_MODEL)),
        ],
        out_specs=pl.BlockSpec((tm, D_MODEL), lambda i: (i, 0)),
        out_shape=jax.ShapeDtypeStruct((t, D_MODEL), F32),
        compiler_params=pltpu.CompilerParams(
            dimension_semantics=("arbitrary",), vmem_limit_bytes=VMEM_LIMIT),
        name="ffn",
    )(x2d, g, w1, w3, w2)


QUARTER = HEAD_DIM // 2


def _quarter_interleave(a, axis=-1):
    axis = axis % a.ndim
    shp = a.shape
    n = shp[axis] // LANES
    a = a.reshape(shp[:axis] + (n, 2, 2, QUARTER) + shp[axis + 1:])
    return jnp.swapaxes(a, axis + 1, axis + 2).reshape(shp)


def _proj_rows(r0, rows, x_ref, g_ref, w_ref, hm_ref, qg_ref, kg_ref, cos_ref, sin_ref,
               q_ref, qi_ref, k_ref, vt_ref, ki_ref, wit_ref):
    rs = slice(r0, r0 + rows)
    h = _rms(x_ref[rs, :], g_ref[...]).astype(BF16)
    z = _dot(h, w_ref[...])
    c32 = cos_ref[rs, :]
    s32 = sin_ref[rs, :]
    cos = jnp.concatenate([c32, c32, c32, c32], axis=1)
    sin = jnp.concatenate([-s32, -s32, s32, s32], axis=1)
    lane = lax.broadcasted_iota(jnp.int32, (rows, LANES), 1)
    is_a = (lane & QUARTER) == 0

    def rope(x):
        return x * cos + pltpu.roll(x, HEAD_DIM, axis=1) * sin

    q = z[:, C_Q:C_Q + ATTN_WIDTH]
    q_ms = _dot((q * q).astype(BF16), hm_ref[...])
    qn = (q * lax.rsqrt(q_ms + EPS)) * qg_ref[...]
    for s in range(ATTN_WIDTH // LANES):
        sl = slice(s * LANES, (s + 1) * LANES)
        q_ref[s, rs, :] = (rope(qn[:, sl]) * Q_SCALE).astype(BF16)
        qi_ref[s, rs, :] = rope(z[:, C_QI + s * LANES:C_QI + (s + 1) * LANES]).astype(BF16)

    kv = z[:, C_KV:C_KV + LANES]
    ms = jnp.sum(jnp.where(is_a, kv * kv, 0.0), axis=-1, keepdims=True) * (1.0 / HEAD_DIM)
    k_a = jnp.where(is_a, rope((kv * lax.rsqrt(ms + EPS)) * kg_ref[...]), 0.0)
    k_ref[0, rs, :] = k_a.astype(BF16)
    k_ref[1, rs, :] = pltpu.roll(k_a, QUARTER, axis=1).astype(BF16)

    v = jnp.where(lane < QUARTER, pltpu.roll(kv, LANES - QUARTER, axis=1),
                  pltpu.roll(kv, HEAD_DIM, axis=1))
    vt_ref[:, rs] = jnp.where(lane < HEAD_DIM, v, 1.0).T.astype(BF16)

    kiw = z[:, C_KIW:C_KIW + LANES]
    ki_a = jnp.where(is_a, rope(kiw), 0.0)
    ki_ref[0, rs, :] = ki_a.astype(BF16)
    ki_ref[1, rs, :] = pltpu.roll(ki_a, QUARTER, axis=1).astype(BF16)
    wit_ref[:, rs] = kiw.T[QUARTER:QUARTER + IDX_HEADS, :]


def _proj_kernel(*refs):
    for r in range(TM_PROJ // SUB_PROJ):
        _proj_rows(r * SUB_PROJ, SUB_PROJ, *refs)


def _proj(x2d, g, w_p, head_mean, qg, kg, cos, sin):
    t, tm = x2d.shape[0], TM_PROJ
    row = lambda i: (i, 0)
    colb = lambda i: (0, i)
    n_pairs = ATTN_WIDTH // LANES
    stack = lambda n: (jax.ShapeDtypeStruct((n, t, LANES), BF16),
                       pl.BlockSpec((n, tm, LANES), lambda i: (0, i, 0)))
    (q_shape, q_spec), (k_shape, k_spec) = stack(n_pairs), stack(2)
    return pl.pallas_call(
        _proj_kernel,
        grid=(t // tm,),
        in_specs=[
            pl.BlockSpec((tm, D_MODEL), row),
            _const_spec((1, D_MODEL)),
            _const_spec((D_MODEL, C_END)),
            _const_spec((ATTN_WIDTH, ATTN_WIDTH)),
            _const_spec((1, ATTN_WIDTH)),
            _const_spec((1, LANES)),
            pl.BlockSpec((tm, QUARTER), row),
            pl.BlockSpec((tm, QUARTER), row),
        ],
        out_specs=[q_spec, q_spec, k_spec, pl.BlockSpec((LANES, tm), colb), k_spec,
                   pl.BlockSpec((IDX_HEADS, tm), colb)],
        out_shape=[q_shape, q_shape, k_shape, jax.ShapeDtypeStruct((LANES, t), BF16), k_shape,
                   jax.ShapeDtypeStruct((IDX_HEADS, t), F32)],
        compiler_params=pltpu.CompilerParams(
            dimension_semantics=("arbitrary",), vmem_limit_bytes=VMEM_LIMIT),
        name="proj",
    )(x2d, g, w_p, head_mean, qg, kg, cos, sin)


def _key_to_f32(key):
    return pltpu.bitcast(key ^ ((key >> 31) & 0x7FFFFFFF), F32)


def _colsum(x):
    n, tq = x.shape
    part = jnp.sum(x.reshape(n // CNT_ROWS, CNT_ROWS, tq), axis=0)
    return jnp.sum(part, axis=0, keepdims=True)


def _dot_nt(a, b):
    return lax.dot_general(a, b, (((1,), (1,)), ((), ())), preferred_element_type=F32)


def _indexer_rows(ki_ref, qi_ref, wit_ref, hd, r0, rows):
    d = _dot_nt(ki_ref[hd & 1, pl.ds(r0, rows), :], qi_ref[hd >> 1])
    return wit_ref[pl.ds(hd, 1), :] * jnp.maximum(d, 0.0)


def _attn_step(lmax_ref, q_ref, qi_ref, wit_ref, qin_ref, witn_ref, k_ref, vt_ref, ki_ref,
               y_ref, sc_a, sc_b, lg_s, bias_s, s16_s, *, j, nq, topk, idx_scale):
    tq = q_ref.shape[1]
    sk = (j + 1) * tq
    sc_cur, sc_nxt = (sc_a, sc_b) if j % 2 == 0 else (sc_b, sc_a)
    has_next = j + 1 < nq
    skn = (j + 2) * tq
    qidx = j * tq + lax.broadcasted_iota(jnp.int32, (1, tq), 1)
    kidx = lax.broadcasted_iota(jnp.int32, (sk, tq), 0)

    if j == 0:
        raw = jnp.zeros((sk, tq), F32)
        for hd in range(IDX_HEADS):
            raw = raw + _indexer_rows(ki_ref, qi_ref, wit_ref, hd, 0, sk)
    else:
        raw = sc_cur[0:sk, :]
    score = jnp.where(kidx <= qidx, raw * idx_scale, -jnp.inf)
    sc_cur[0:sk, :] = score
    s16_s[0:sk, :] = score.astype(BF16)
    if has_next:
        sc_nxt[0:skn, :] = jnp.zeros((skn, tq), F32)

    kf = float(topk)

    half = tq // 2
    live = sk - half

    def count_rows(ref, cand, dtype):
        acc = jnp.zeros((CNT_ROWS, tq), dtype)
        for r in range(0, live, CNT_ROWS):
            acc = jnp.where(ref[r:r + CNT_ROWS, :] >= cand, acc + 1.0, acc)
        hi, cand_hi = acc[:, half:], cand[:, half:]
        for r in range(live, sk, CNT_ROWS):
            hi = jnp.where(ref[r:r + CNT_ROWS, half:] >= cand_hi, hi + 1.0, hi)
        acc = jnp.concatenate([acc[:, :half], hi], axis=1)
        return jnp.sum(acc.astype(F32), axis=0, keepdims=True)

    def count_ge(cand):
        return count_rows(sc_cur, cand, F32)

    def count_ge16(cand):
        cand = jnp.broadcast_to(cand.astype(BF16), (CNT_ROWS, tq))
        return count_rows(s16_s, cand, BF16)

    def mxu_slice(i):
        pair, part = i >> 1, i & 1
        r0 = pl.multiple_of(part * (sk // 2), LANES)
        qs = q_ref[pair]
        for v in range(2):
            lg_s[2 * pair + v, pl.ds(r0, sk // 2), :] = _dot_nt(
                k_ref[v, pl.ds(r0, sk // 2), :], qs)
        if has_next:
            r0n = pl.multiple_of(part * (skn // 2), LANES)
            sc_nxt[pl.ds(r0n, skn // 2), :] += (
                _indexer_rows(ki_ref, qin_ref, witn_ref, 2 * pair, r0n, skn // 2)
                + _indexer_rows(ki_ref, qin_ref, witn_ref, 2 * pair + 1, r0n, skn // 2))

    def probe(count, t, bit):
        cand = t + jnp.left_shift(jnp.int32(1), bit)
        return jnp.where(count(_key_to_f32(cand)) >= kf, cand, t)

    def probe_counted(carry, bit):
        t, cnt = carry
        cand = t + jnp.left_shift(jnp.int32(1), bit)
        c = count_ge(_key_to_f32(cand))
        return jnp.where(c >= kf, cand, t), jnp.where(c >= kf, c, cnt)

    def level1(i, t):
        return probe(count_ge16, probe(count_ge16, t, 31 - 2 * i), 30 - 2 * i)

    def level2(i, carry):
        carry = probe_counted(probe_counted(carry, 16 - 2 * i), 15 - 2 * i)
        mxu_slice(i)
        return carry

    if sk <= topk:
        for i in range(N_HEADS):
            mxu_slice(i)
        tkey = jnp.full((1, tq), KEY_NEG_INF, jnp.int32)
        n_ge = jnp.zeros((1, tq), F32)
    else:
        t16 = lax.fori_loop(0, N_HEADS, level1, jnp.full((1, tq), INT_MIN, jnp.int32))
        base = jnp.maximum(t16, INT_MIN + BF16_STEP) - BF16_STEP
        carry = lax.fori_loop(0, N_HEADS, level2, (base, jnp.full((1, tq), kf + 1.0, F32)))
        tkey, n_ge = probe_counted(carry, 0)
        tkey = jnp.maximum(tkey, KEY_NEG_INF)
    thr = _key_to_f32(tkey)

    tie = jnp.where(tkey > KEY_NEG_INF, jnp.where(n_ge > kf, 1.0, 0.0), 0.0)
    any_tie = jnp.max(tie) > 0.0

    def tie_cut():
        sc = sc_cur[0:sk, :]
        need = kf - _colsum(jnp.where(sc > thr, 1.0, 0.0))
        eq = jnp.where(sc == thr, 1.0, 0.0)

        def step(i, c):
            cand = c + jnp.left_shift(jnp.int32(1), idx_bits - 1 - i)
            cnt = _colsum(jnp.where(kidx < cand, eq, 0.0))
            return jnp.where(cnt < need, cand, c)

        idx_bits = sk.bit_length()
        return lax.fori_loop(0, idx_bits, step, jnp.zeros((1, tq), jnp.int32))

    cut = lax.cond(any_tie, tie_cut, lambda: jnp.full((1, tq), sk, jnp.int32))
    cut = jnp.minimum(cut, qidx)
    bounded = lmax_ref[0] <= MAX_SAFE_LOGIT
    shift = jnp.where(bounded, lmax_ref[0], 0.0)
    sc = sc_cur[0:sk, :]
    bias_s[0:sk, :] = jnp.where(
        sc > thr, -shift, jnp.where(sc == thr, jnp.where(kidx <= cut, -shift, NEG_BIG), NEG_BIG))

    @pl.when(jnp.logical_not(bounded))
    def _():
        def per_head(hd, carry):
            def blk(ci):
                return pl.ds(pl.multiple_of(ci * tq, tq), tq)

            def row_max(ci, m):
                s = lg_s[hd, blk(ci), :] + bias_s[blk(ci), :]
                return jnp.maximum(m, jnp.max(s, axis=0, keepdims=True))

            m = lax.fori_loop(0, j + 1, row_max, jnp.full((1, tq), -jnp.inf, F32))

            def sub(ci, c):
                lg_s[hd, blk(ci), :] = lg_s[hd, blk(ci), :] - m
                return c

            return lax.fori_loop(0, j + 1, sub, carry)

        lax.fori_loop(0, N_HEADS, per_head, 0)

    vt = vt_ref[:, 0:sk]
    outs = []
    for hd in range(N_HEADS):
        s = lg_s[hd, 0:sk, :] + bias_s[0:sk, :]
        o = _dot(vt, jnp.exp2(s).astype(BF16))
        outs.append(o[0:HEAD_DIM, :] / o[HEAD_DIM:HEAD_DIM + 1, :])
    y_ref[...] = jnp.concatenate(outs, axis=0).T.astype(BF16)


def _attn_kernel(*refs, nq, topk, idx_scale):
    j = pl.program_id(1)
    for c in range(nq):
        pl.when(j == c)(functools.partial(
            _attn_step, *refs, j=c, nq=nq, topk=topk, idx_scale=idx_scale))


def _attn(logit_bound, q, qi, wit, k, vt, ki, *, batch, seq, topk):
    tq = TQ_ATTN
    assert IDX_HEADS == N_HEADS and seq % tq == 0 and tq % LANES == 0
    nq = seq // tq
    n_pairs = ATTN_WIDTH // LANES
    blk = lambda b, j: b * nq + j
    nblk = lambda b, j: b * nq + jnp.minimum(j + 1, nq - 1)
    qspec = lambda f: pl.BlockSpec((n_pairs, tq, LANES), lambda b, j: (0, f(b, j), 0))
    wspec = lambda f: pl.BlockSpec((IDX_HEADS, tq), lambda b, j: (0, f(b, j)))
    kspec = pl.BlockSpec((2, seq, LANES), lambda b, j: (0, b, 0))
    return pl.pallas_call(
        functools.partial(_attn_kernel, nq=nq, topk=topk,
                          idx_scale=(IDX_DIM ** -0.5) * (IDX_HEADS ** -0.5)),
        grid=(batch, nq),
        in_specs=[
            pl.BlockSpec(memory_space=pltpu.SMEM),
            qspec(blk), qspec(blk), wspec(blk), qspec(nblk), wspec(nblk),
            kspec,
            pl.BlockSpec((LANES, seq), lambda b, j: (0, b)),
            kspec,
        ],
        out_specs=pl.BlockSpec((tq, ATTN_WIDTH), lambda b, j: (blk(b, j), 0)),
        out_shape=jax.ShapeDtypeStruct((batch * seq, ATTN_WIDTH), BF16),
        scratch_shapes=[pltpu.VMEM((seq, tq), F32), pltpu.VMEM((seq, tq), F32),
                        pltpu.VMEM((N_HEADS, seq, tq), F32),
                        pltpu.VMEM((seq, tq), F32), pltpu.VMEM((seq, tq), BF16)],
        compiler_params=pltpu.CompilerParams(
            dimension_semantics=("arbitrary", "arbitrary"), vmem_limit_bytes=VMEM_LIMIT),
        name="attn",
    )(logit_bound, q, qi, wit, qi, wit, k, vt, ki)


def _merge_rows(r0, rows, t, x_ref, ya_ref, g_ref, wug_ref, pw_ref, pb_ref, ps_ref, pa_ref, pp_ref,
                wo_ref, o_ref, ubuf):
    rs = slice(r0, r0 + rows)
    x = x_ref[rs, :]
    h = _rms(x, g_ref[...]).astype(BF16)
    z = _dot(h, wug_ref[...])
    u = z[:, 0:POOL_WIDTH]
    g_attn = z[:, POOL_WIDTH:POOL_WIDTH + D_MODEL]
    g_pool = z[:, POOL_WIDTH + D_MODEL:POOL_WIDTH + 2 * D_MODEL]

    base = MAX_WIN + r0
    ubuf[base:base + rows, :] = u
    tm = x_ref.shape[0]
    pos1 = (t * tm + r0 + 1 + lax.broadcasted_iota(jnp.int32, (rows, 1), 0)).astype(F32)
    pb = pb_ref[...]
    ps = ps_ref[...]
    y_pool = []
    for gi, win in enumerate(POOL_WINDOWS):
        sl = slice(gi * POOL_GROUP_DIM, (gi + 1) * POOL_GROUP_DIM)
        ug = u[:, sl]
        acc = ug
        for s in range(1, win):
            acc = acc + ubuf[base - s:base - s + rows, sl]
        pooled = (acc / jnp.minimum(pos1, float(win)) - ug).astype(BF16)
        mixed = _dot(pooled, pw_ref[gi]) + pb[:, sl]
        y_pool.append((mixed * ps[:, sl]).astype(BF16))
    y_pool = jnp.concatenate(y_pool, axis=1)

    merged = (jax.nn.sigmoid(g_attn) * _dot(ya_ref[rs, :], pa_ref[...])
              + jax.nn.sigmoid(g_pool) * _dot(y_pool, pp_ref[...]))
    o_ref[rs, :] = x + _dot(merged.astype(BF16), wo_ref[...])


def _merge_kernel(*refs):
    ubuf = refs[-1]
    tm = refs[0].shape[0]
    t = pl.program_id(1)

    @pl.when(t == 0)
    def _():
        ubuf[0:MAX_WIN, :] = jnp.zeros((MAX_WIN, POOL_WIDTH), F32)

    for r in range(0, tm, SUB_MERGE):
        _merge_rows(r, SUB_MERGE, t, *refs)
    ubuf[0:MAX_WIN, :] = ubuf[tm:tm + MAX_WIN, :]


def _merge(x2d, y_attn, g, w_ug, pool_w, pool_b, pool_scale, proj_attn, proj_pool, w_out,
           *, batch, seq):
    tm = TM_MERGE
    nt = seq // tm
    row = lambda b, i: (b * nt + i, 0)
    return pl.pallas_call(
        _merge_kernel,
        grid=(batch, nt),
        in_specs=[
            pl.BlockSpec((tm, D_MODEL), row),
            pl.BlockSpec((tm, ATTN_WIDTH), row),
            _const_spec((1, D_MODEL)),
            _const_spec((D_MODEL, POOL_WIDTH + 2 * D_MODEL)),
            _const_spec((POOL_GROUPS, POOL_GROUP_DIM, POOL_GROUP_DIM)),
            _const_spec((1, POOL_WIDTH)),
            _const_spec((1, POOL_WIDTH)),
            _const_spec((ATTN_WIDTH, D_MODEL)),
            _const_spec((POOL_WIDTH, D_MODEL)),
            _const_spec((D_MODEL, D_MODEL)),
        ],
        out_specs=pl.BlockSpec((tm, D_MODEL), row),
        out_shape=jax.ShapeDtypeStruct((batch * seq, D_MODEL), F32),
        scratch_shapes=[pltpu.VMEM((MAX_WIN + tm, POOL_WIDTH), F32)],
        compiler_params=pltpu.CompilerParams(
            dimension_semantics=("arbitrary", "arbitrary"), vmem_limit_bytes=VMEM_LIMIT),
        name="merge",
    )(x2d, y_attn, g, w_ug, pool_w, pool_b, pool_scale, proj_attn, proj_pool, w_out)


def _rope_tables(positions):
    inv_freq = ROPE_THETA ** (-jnp.arange(0, HEAD_DIM, 2, dtype=F32) / HEAD_DIM)
    per_row = LANES // QUARTER
    ang = (positions.astype(F32).reshape(-1, per_row, 1) * inv_freq).reshape(-1, LANES)
    cos, sin = lax.optimization_barrier((jnp.cos(ang), jnp.sin(ang)))
    return cos.reshape(-1, QUARTER), sin.reshape(-1, QUARTER)


def kernel(x, positions, ffn1_norm, ffn1_w1, ffn1_w3, ffn1_w2, mix_norm, w_in, q_norm, k_norm,
           pool_w, pool_b, pool_scale, proj_attn, proj_pool, w_out,
           ffn2_norm, ffn2_w1, ffn2_w3, ffn2_w2):
    batch, seq, _ = x.shape
    depth = ffn1_norm.shape[0]
    topk = min(TOPK_MAX, seq // 4)
    cos, sin = _rope_tables(positions)
    head_of_lane = _quarter_interleave(np.repeat(np.arange(N_HEADS), HEAD_DIM))
    head_mean = jnp.asarray(
        (head_of_lane[:, None] == head_of_lane[None, :]) * (1.0 / HEAD_DIM), BF16)
    bf = lambda w: w.astype(BF16)

    xt = x.reshape(batch * seq, D_MODEL)
    for l in range(depth):
        w_proj = _quarter_interleave(bf(w_in[l, :, 0:C_END]))
        w_ug = bf(w_in[l, :, C_U:])
        qg = _quarter_interleave(jnp.tile(q_norm[l], N_HEADS))[None]
        kg = _quarter_interleave(jnp.concatenate([k_norm[l], jnp.ones_like(k_norm[l])]))[None]
        xt = _ffn(xt, ffn1_norm[l][None], bf(ffn1_w1[l]), bf(ffn1_w3[l]), bf(ffn1_w2[l]))
        q, qi, k, vt, ki, wit = _proj(xt, mix_norm[l][None], w_proj, head_mean, qg, kg, cos, sin)
        logit_bound = (BOUND_MARGIN * HEAD_DIM * Q_SCALE * jnp.max(jnp.abs(q_norm[l]))
                       * jnp.max(jnp.abs(k_norm[l]))).reshape(1)
        y_attn = _attn(logit_bound, q, qi, wit, k, vt, ki, batch=batch, seq=seq, topk=topk)
        xt = _merge(xt, y_attn, mix_norm[l][None], w_ug, bf(pool_w[l]),
                    pool_b[l].reshape(1, POOL_WIDTH), pool_scale[l][None],
                    bf(proj_attn[l]), bf(proj_pool[l]), bf(w_out[l]), batch=batch, seq=seq)
        xt = _ffn(xt, ffn2_norm[l][None], bf(ffn2_w1[l]), bf(ffn2_w3[l]), bf(ffn2_w2[l]))
    return xt.reshape(batch, seq, D_MODEL)
```
